```python
import math
import jax, jax.numpy as jnp
from jax import lax
import numpy as np

D_MODEL = 2048
BATCH = 4
SEQ = 4096
DEPTH = 2
DEC_BATCH = 2
DEC_SEQ = 4096
PAST_LEN = 128

S5_WIDTH = D_MODEL // 2
S5_GROUP = 16
S5_GROUPS = S5_WIDTH // S5_GROUP
S5_STATE = 64
HEAD_DIM = 128
N_Q_HEADS = (D_MODEL // 2) // HEAD_DIM
N_KV_HEADS = 2
Q_PER_KV = N_Q_HEADS // N_KV_HEADS
ATTN_WIDTH = N_Q_HEADS * HEAD_DIM
KV_WIDTH = N_KV_HEADS * HEAD_DIM
WINDOW = 128
BLOCK = 128
ROPE_DIM = HEAD_DIM // 4
ROPE_THETA = 500000.0
EVEN_IN = S5_WIDTH + ATTN_WIDTH + 2 * KV_WIDTH
EVEN_MIX = S5_WIDTH + ATTN_WIDTH
LRU_WIDTH = D_MODEL
LRU_BLOCKS = 8
LRU_BS = LRU_WIDTH // LRU_BLOCKS
CONV_WIDTH = 4
CONV_LEFT = CONV_WIDTH // 2
LRU_C = 8.0
D_FF = ((8 * D_MODEL + 3 * 256 - 1) // (3 * 256)) * 256
N_EVEN = (DEPTH + 1) // 2
N_ODD = DEPTH // 2
EPS = 1e-6
NEG_INF = -1e30

kernel_name = "hybrid_s5_swa_rglru_encoder"


def rmsnorm(x, g):
    xf = x.astype(jnp.float32)
    y = xf * lax.rsqrt(jnp.mean(xf * xf, axis=-1, keepdims=True) + EPS)
    return (y * g.astype(jnp.float32)).astype(x.dtype)


def rope_partial(x, pos):
    half = ROPE_DIM // 2
    inv = ROPE_THETA ** (-jnp.arange(half, dtype=jnp.float32) / half)
    ang = pos.astype(jnp.float32)[:, None] * inv[None, :]
    cos = jnp.cos(ang)[None, :, None, :]
    sin = jnp.sin(ang)[None, :, None, :]
    xr = x[..., :ROPE_DIM].astype(jnp.float32)
    x1, x2 = xr[..., :half], xr[..., half:]
    rot = jnp.concatenate([x1 * cos - x2 * sin, x2 * cos + x1 * sin], axis=-1)
    return jnp.concatenate([rot.astype(x.dtype), x[..., ROPE_DIM:]], axis=-1)


def window_attention(q, k, v, sink):
    Bsz, L = q.shape[0], q.shape[1]
    nb = L // BLOCK
    qb = q.reshape(Bsz, nb, BLOCK, N_KV_HEADS, Q_PER_KV, HEAD_DIM)
    pad = ((0, 0), (BLOCK, BLOCK), (0, 0), (0, 0))
    kp = jnp.pad(k, pad).reshape(Bsz, nb + 2, BLOCK, N_KV_HEADS, HEAD_DIM)
    vp = jnp.pad(v, pad).reshape(Bsz, nb + 2, BLOCK, N_KV_HEADS, HEAD_DIM)
    kb = jnp.concatenate([kp[:, :-2], kp[:, 1:-1], kp[:, 2:]], axis=2)
    vb = jnp.concatenate([vp[:, :-2], vp[:, 1:-1], vp[:, 2:]], axis=2)
    s = jnp.einsum('bnqhgd,bnkhd->bnhgqk', qb, kb,
                   preferred_element_type=jnp.float32) * (HEAD_DIM ** -0.5)
    qi = jnp.arange(BLOCK)[:, None]
    kj = jnp.arange(3 * BLOCK)[None, :]
    off = kj - BLOCK - qi
    kpos = (jnp.arange(nb)[:, None, None] - 1) * BLOCK + kj[None]
    valid = (jnp.abs(off) <= WINDOW)[None] & (kpos >= 0) & (kpos < L)
    s = jnp.where(valid[None, :, None, None], s, NEG_INF)
    sk = sink.astype(jnp.float32).reshape(1, 1, N_KV_HEADS, Q_PER_KV, 1, 1)
    m = jnp.maximum(jnp.max(s, axis=-1, keepdims=True), sk)
    p = jnp.exp(s - m)
    denom = jnp.sum(p, axis=-1, keepdims=True) + jnp.exp(sk - m)
    o = jnp.einsum('bnhgqk,bnkhd->bnqhgd', (p / denom).astype(v.dtype), vb)
    return o.reshape(Bsz, L, ATTN_WIDTH)


def _complex_combine(e1, e2):
    a1r, a1i, b1r, b1i = e1
    a2r, a2i, b2r, b2i = e2
    return (a1r * a2r - a1i * a2i,
            a1r * a2i + a1i * a2r,
            a2r * b1r - a2i * b1i + b2r,
            a2r * b1i + a2i * b1r + b2i)


def _real_combine(e1, e2):
    a1, b1 = e1
    a2, b2 = e2
    return (a1 * a2, a2 * b1 + b2)


def s5_mixer(u, lam_re, lam_im, log_dt, b_re, b_im, c_re, c_im, d_skip, w_glu, b_glu):
    Bsz, L = u.shape[0], u.shape[1]
    uf = u.astype(jnp.float32).reshape(Bsz, L, S5_GROUPS, S5_GROUP)
    lr = jnp.minimum(lam_re.astype(jnp.float32), -1e-4)
    li = lam_im.astype(jnp.float32)
    dt = jnp.exp(log_dt.astype(jnp.float32))[..., None]
    mag = jnp.exp(lr * dt)
    ab_re = mag * jnp.cos(li * dt)
    ab_im = mag * jnp.sin(li * dt)
    nr, ni = ab_re - 1.0, ab_im
    den = lr * lr + li * li
    f_re = (nr * lr + ni * li) / den
    f_im = (ni * lr - nr * li) / den
    br, bi = b_re.astype(jnp.float32), b_im.astype(jnp.float32)
    bb_re = f_re[..., None] * br - f_im[..., None] * bi
    bb_im = f_re[..., None] * bi + f_im[..., None] * br
    y = d_skip.astype(jnp.float32).reshape(S5_GROUPS, S5_GROUP) * uf
    for z, rev in ((0, False), (1, True)):
        bu_re = jnp.einsum('blgn,gpn->blgp', uf, bb_re[z])
        bu_im = jnp.einsum('blgn,gpn->blgp', uf, bb_im[z])
        a_re = jnp.broadcast_to(ab_re[z], bu_re.shape)
        a_im = jnp.broadcast_to(ab_im[z], bu_re.shape)
        _, _, h_re, h_im = lax.associative_scan(
            _complex_combine, (a_re, a_im, bu_re, bu_im), reverse=rev, axis=1)
        y = y + (jnp.einsum('blgp,gnp->blgn', h_re, c_re[z].astype(jnp.float32))
                 - jnp.einsum('blgp,gnp->blgn', h_im, c_im[z].astype(jnp.float32)))
    g = jax.nn.gelu(y.reshape(Bsz, L, S5_WIDTH))
    out = g * jax.nn.sigmoid(g @ w_glu.astype(jnp.float32) + b_glu.astype(jnp.float32))
    return out.astype(u.dtype)


def rglru_mixer(xb, conv_w, conv_b, wa, ba, wx, bx, lam):
    Bsz, L = xb.shape[0], xb.shape[1]
    xf = xb.astype(jnp.float32)
    xp = jnp.pad(xf, ((0, 0), (CONV_LEFT, CONV_WIDTH - 1 - CONV_LEFT), (0, 0)))
    cw = conv_w.astype(jnp.float32)
    conv = conv_b.astype(jnp.float32) + sum(xp[:, t:t + L] * cw[t] for t in range(CONV_WIDTH))
    cblk = conv.reshape(Bsz, L, LRU_BLOCKS, LRU_BS)
    h_sum = jnp.zeros_like(conv)
    for z, rev in ((0, False), (1, True)):
        r = jax.nn.sigmoid(jnp.einsum('blni,nij->blnj', cblk, wa[z].astype(jnp.float32))
                           .reshape(Bsz, L, LRU_WIDTH) + ba[z].astype(jnp.float32))
        i = jax.nn.sigmoid(jnp.einsum('blni,nij->blnj', cblk, wx[z].astype(jnp.float32))
                           .reshape(Bsz, L, LRU_WIDTH) + bx[z].astype(jnp.float32))
        log_a = -LRU_C * r * jax.nn.softplus(-lam[z].astype(jnp.float32))
        a = jnp.exp(log_a)
        b = jnp.sqrt(-jnp.expm1(2.0 * log_a)) * (i * conv)
        _, h = lax.associative_scan(_real_combine, (a, b), reverse=rev, axis=1)
        h_sum = h_sum + h
    return h_sum.astype(xb.dtype)


def setup_inputs(seed: int = 0) -> dict:
    key = jax.random.key(seed)
    ks = iter(jax.random.split(key, 48))
    f32 = jnp.float32

    def nrm(shape, scale):
        return jax.random.normal(next(ks), shape, f32) * scale

    def gain(shape):
        return 1.0 + nrm(shape, 0.02)

    lam_a0 = jax.random.uniform(next(ks), (N_ODD, 2, LRU_WIDTH), f32, minval=0.9, maxval=0.999)
    lam_s = lam_a0 ** (1.0 / LRU_C)
    return {
        "x_prompt": nrm((BATCH, SEQ, D_MODEL), 1.0),
        "x_sample": nrm((DEC_BATCH, DEC_SEQ, D_MODEL), 1.0),
        "norm_mix": gain((DEPTH, D_MODEL)),
        "norm_ffn": gain((DEPTH, D_MODEL)),
        "ev_w_in": nrm((N_EVEN, D_MODEL, EVEN_IN), D_MODEL ** -0.5),
        "ev_w_out": nrm((N_EVEN, EVEN_MIX, D_MODEL), EVEN_MIX ** -0.5),
        "s5_lam_re": -0.5 * jnp.exp(nrm((N_EVEN, 2, S5_GROUPS, S5_STATE), 0.05)),
        "s5_lam_im": jnp.pi * jnp.arange(S5_STATE, dtype=f32) + nrm((N_EVEN, 2, S5_GROUPS, S5_STATE), 0.01),
        "s5_log_dt": jax.random.uniform(next(ks), (N_EVEN, 2, S5_GROUPS), f32,
                                        minval=math.log(1e-3), maxval=math.log(1e-1)),
        "s5_b_re": nrm((N_EVEN, 2, S5_GROUPS, S5_STATE, S5_GROUP), (2 * S5_GROUP) ** -0.5),
        "s5_b_im": nrm((N_EVEN, 2, S5_GROUPS, S5_STATE, S5_GROUP), (2 * S5_GROUP) ** -0.5),
        "s5_c_re": nrm((N_EVEN, 2, S5_GROUPS, S5_GROUP, S5_STATE), S5_STATE ** -0.5),
        "s5_c_im": nrm((N_EVEN, 2, S5_GROUPS, S5_GROUP, S5_STATE), S5_STATE ** -0.5),
        "s5_d": nrm((N_EVEN, S5_WIDTH), 1.0),
        "s5_w_glu": nrm((N_EVEN, S5_WIDTH, S5_WIDTH), S5_WIDTH ** -0.5),
        "s5_b_glu": nrm((N_EVEN, S5_WIDTH), 0.01),
        "attn_q_norm": gain((N_EVEN, HEAD_DIM)),
        "attn_k_norm": gain((N_EVEN, HEAD_DIM)),
        "attn_sink": nrm((N_EVEN, N_Q_HEADS), 0.5),
        "od_w_in": nrm((N_ODD, D_MODEL, 2 * LRU_WIDTH), D_MODEL ** -0.5),
        "od_w_out": nrm((N_ODD, LRU_WIDTH, D_MODEL), LRU_WIDTH ** -0.5),
        "lru_conv_w": nrm((N_ODD, CONV_WIDTH, LRU_WIDTH), CONV_WIDTH ** -0.5),
        "lru_conv_b": nrm((N_ODD, LRU_WIDTH), 0.01),
        "lru_wa": nrm((N_ODD, 2, LRU_BLOCKS, LRU_BS, LRU_BS), LRU_BS ** -0.5),
        "lru_ba": nrm((N_ODD, 2, LRU_WIDTH), 0.01),
        "lru_wx": nrm((N_ODD, 2, LRU_BLOCKS, LRU_BS, LRU_BS), LRU_BS ** -0.5),
        "lru_bx": nrm((N_ODD, 2, LRU_WIDTH), 0.01),
        "lru_lam": jnp.log(lam_s) - jnp.log1p(-lam_s),
        "ffn_w1": nrm((DEPTH, D_MODEL, D_FF), D_MODEL ** -0.5),
        "ffn_w3": nrm((DEPTH, D_MODEL, D_FF), D_MODEL ** -0.5),
        "ffn_w2": nrm((DEPTH, D_FF, D_MODEL), D_FF ** -0.5),
    }


def reference(x_prompt, x_sample, norm_mix, norm_ffn, ev_w_in, ev_w_out,
              s5_lam_re, s5_lam_im, s5_log_dt, s5_b_re, s5_b_im, s5_c_re, s5_c_im,
              s5_d, s5_w_glu, s5_b_glu, attn_q_norm, attn_k_norm, attn_sink,
              od_w_in, od_w_out, lru_conv_w, lru_conv_b, lru_wa, lru_ba, lru_wx, lru_bx, lru_lam,
              ffn_w1, ffn_w3, ffn_w2):
    q0 = S5_WIDTH
    k0 = q0 + ATTN_WIDTH
    v0 = k0 + KV_WIDTH

    def run(x):
        Bsz, L = x.shape[0], x.shape[1]
        pos = jnp.arange(L)
        for layer in range(DEPTH):
            h = rmsnorm(x, norm_mix[layer])
            if layer % 2 == 0:
                e = layer // 2
                proj = h @ ev_w_in[e]
                u = proj[..., :q0]
                q = proj[..., q0:k0].reshape(Bsz, L, N_Q_HEADS, HEAD_DIM)
                k = proj[..., k0:v0].reshape(Bsz, L, N_KV_HEADS, HEAD_DIM)
                v = proj[..., v0:].reshape(Bsz, L, N_KV_HEADS, HEAD_DIM)
                q = rope_partial(rmsnorm(q, attn_q_norm[e]), pos)
                k = rope_partial(rmsnorm(k, attn_k_norm[e]), pos)
                y_attn = window_attention(q, k, v, attn_sink[e])
                y_s5 = s5_mixer(u, s5_lam_re[e], s5_lam_im[e], s5_log_dt[e], s5_b_re[e], s5_b_im[e],
                                s5_c_re[e], s5_c_im[e], s5_d[e], s5_w_glu[e], s5_b_glu[e])
                mix = jnp.concatenate([y_s5, y_attn], axis=-1)
                x = x + mix @ ev_w_out[e]
            else:
                o = layer // 2
                proj = h @ od_w_in[o]
                gate = proj[..., :LRU_WIDTH]
                xb = proj[..., LRU_WIDTH:]
                y_rec = rglru_mixer(xb, lru_conv_w[o], lru_conv_b[o], lru_wa[o], lru_ba[o],
                                    lru_wx[o], lru_bx[o], lru_lam[o])
                x = x + (y_rec * jax.nn.gelu(gate)) @ od_w_out[o]
            h = rmsnorm(x, norm_ffn[layer])
            x = x + (jax.nn.silu(h @ ffn_w1[layer]) * (h @ ffn_w3[layer])) @ ffn_w2[layer]
        return x

    y_prompt = run(x_prompt)
    y_sample = run(x_sample)
    return (y_prompt, y_sample)
```

```python
import functools
import math

import jax
import jax.numpy as jnp
from jax import lax
from jax.experimental import pallas as pl
from jax.experimental.pallas import tpu as pltpu

F32 = jnp.float32
BF16 = jnp.bfloat16

LANES = 128
SUBLANES = 8
VMEM_LIMIT_BYTES = 56 * 1024 * 1024

EPS = 1e-6
NEG_INF = -1e30
S5_GROUP = 16
S5_STATE = 64
HEAD_DIM = 128
N_KV_HEADS = 2
Q_PER_KV = 4
WINDOW = 128
BLOCK = 128
ROPE_HALF = 16
ROPE_THETA = 500000.0
LRU_BS = 256
LRU_C = 8.0
CONV_WIDTH = 4
CONV_LEFT = 2

S5_CHUNK = 16
S5_VEC = S5_CHUNK * S5_GROUP
S5_HALF = 2 * S5_STATE


def _params(*semantics):
    return pltpu.CompilerParams(dimension_semantics=semantics, vmem_limit_bytes=VMEM_LIMIT_BYTES)


def _rows_to_tile(rows, reverse):
    n = rows[0].shape[1]
    sub = lax.broadcasted_iota(jnp.int32, (SUBLANES, n), 0)
    tile = jnp.broadcast_to(rows[0], (SUBLANES, n))
    for j in range(1, SUBLANES):
        tile = jnp.where(sub == (SUBLANES - 1 - j if reverse else j), jnp.broadcast_to(rows[j], (SUBLANES, n)), tile)
    return tile


def _rms_normalize(x, gain):
    ms = jnp.mean(x * x, axis=-1, keepdims=True)
    return x * lax.rsqrt(ms + EPS) * gain


def _norm_matmul_kernel(x_ref, g_ref, w_ref, o_ref, hn_ref):
    @pl.when(pl.program_id(1) == 0)
    def _():
        hn_ref[...] = _rms_normalize(x_ref[...], g_ref[...]).astype(BF16)

    o_ref[...] = jnp.dot(hn_ref[...], w_ref[...], preferred_element_type=F32)


def _norm_matmul(x, gain, w, tm, tn):
    m, d = x.shape
    n = w.shape[1]
    return pl.pallas_call(
        _norm_matmul_kernel,
        grid=(m // tm, n // tn),
        in_specs=[
            pl.BlockSpec((tm, d), lambda i, j: (i, 0)),
            pl.BlockSpec((1, d), lambda i, j: (0, 0)),
            pl.BlockSpec((d, tn), lambda i, j: (0, j)),
        ],
        out_specs=pl.BlockSpec((tm, tn), lambda i, j: (i, j)),
        out_shape=jax.ShapeDtypeStruct((m, n), F32),
        scratch_shapes=[pltpu.VMEM((tm, d), BF16)],
        compiler_params=_params("arbitrary", "arbitrary"),
        name="norm_matmul",
    )(x, gain.reshape(1, d), w)


def _proj_residual_kernel(*refs, n_in):
    x_ref = refs[0]
    a_refs = refs[1:1 + n_in]
    w_refs = refs[1 + n_in:1 + 2 * n_in]
    o_ref = refs[1 + 2 * n_in]
    acc = x_ref[...]
    for a_ref, w_ref in zip(a_refs, w_refs):
        acc = acc + jnp.dot(a_ref[...], w_ref[...], preferred_element_type=F32)
    o_ref[...] = acc


def _proj_residual(x, acts, weights, tm, tn):
    m, n = x.shape
    n_in = len(acts)
    in_specs = [pl.BlockSpec((tm, tn), lambda i, j: (i, j))]
    in_specs += [pl.BlockSpec((tm, a.shape[1]), lambda i, j: (i, 0)) for a in acts]
    in_specs += [pl.BlockSpec((w.shape[0], tn), lambda i, j: (0, j)) for w in weights]
    return pl.pallas_call(
        functools.partial(_proj_residual_kernel, n_in=n_in),
        grid=(m // tm, n // tn),
        in_specs=in_specs,
        out_specs=pl.BlockSpec((tm, tn), lambda i, j: (i, j)),
        out_shape=jax.ShapeDtypeStruct((m, n), F32),
        compiler_params=_params("arbitrary", "arbitrary"),
        name="proj_residual",
    )(x, *acts, *weights)


def _ffn_kernel(x_ref, g_ref, w1_ref, w3_ref, w2_ref, o_ref, hn_ref):
    j = pl.program_id(1)

    @pl.when(j == 0)
    def _():
        hn_ref[...] = _rms_normalize(x_ref[...], g_ref[...]).astype(BF16)

    hn = hn_ref[...]
    a = jnp.dot(hn, w1_ref[...], preferred_element_type=F32)
    b = jnp.dot(hn, w3_ref[...], preferred_element_type=F32)
    act = (a * jax.nn.sigmoid(a) * b).astype(BF16)
    contrib = jnp.dot(act, w2_ref[...], preferred_element_type=F32)

    @pl.when(j == 0)
    def _():
        o_ref[...] = x_ref[...] + contrib

    @pl.when(j > 0)
    def _():
        o_ref[...] += contrib


def _ffn(x, gain, w1, w3, w2, tm, tf):
    m, d = x.shape
    f = w1.shape[1]
    return pl.pallas_call(
        _ffn_kernel,
        grid=(m // tm, f // tf),
        in_specs=[
            pl.BlockSpec((tm, d), lambda i, j: (i, 0)),
            pl.BlockSpec((1, d), lambda i, j: (0, 0)),
            pl.BlockSpec((d, tf), lambda i, j: (0, j)),
            pl.BlockSpec((d, tf), lambda i, j: (0, j)),
            pl.BlockSpec((tf, d), lambda i, j: (j, 0)),
        ],
        out_specs=pl.BlockSpec((tm, d), lambda i, j: (i, 0)),
        out_shape=jax.ShapeDtypeStruct((m, d), F32),
        scratch_shapes=[pltpu.VMEM((tm, d), BF16)],
        compiler_params=_params("arbitrary", "arbitrary"),
        name="ffn",
    )(x, gain.reshape(1, d), w1, w3, w2)


def _glu_kernel(y_ref, w_ref, b_ref, o_ref):
    g = jax.nn.gelu(y_ref[...])
    z = jnp.dot(g.astype(BF16), w_ref[...], preferred_element_type=F32) + b_ref[...]
    o_ref[...] = (g * jax.nn.sigmoid(z)).astype(o_ref.dtype)


def _glu(y, w, b, tm):
    m, c = y.shape
    return pl.pallas_call(
        _glu_kernel,
        grid=(m // tm,),
        in_specs=[
            pl.BlockSpec((tm, c), lambda i: (i, 0)),
            pl.BlockSpec((c, c), lambda i: (0, 0)),
            pl.BlockSpec((1, c), lambda i: (0, 0)),
        ],
        out_specs=pl.BlockSpec((tm, c), lambda i: (i, 0)),
        out_shape=jax.ShapeDtypeStruct((m, c), BF16),
        compiler_params=_params("arbitrary"),
        name="s5_glu",
    )(y, w, b.reshape(1, c))


def _s5_kernel(u_ref, w_ref, m_ref, v_ref, a_ref, y_ref, s_ref, hf_ref, hb_ref, *, n_seq, n_chunk):
    u = u_ref[...]
    s_ref[...] = jnp.dot(u, w_ref[...], preferred_element_type=F32)
    ar = a_ref[0:1, :]
    ai = a_ref[1:2, :]
    is_fwd = lax.broadcasted_iota(jnp.int32, (1, S5_HALF), 1) < S5_STATE

    def step(k, carry):
        lo = pl.multiple_of(k * SUBLANES, SUBLANES)
        hi = pl.multiple_of(n_chunk - SUBLANES - k * SUBLANES, SUBLANES)
        out = []
        for b in range(n_seq):
            hre, him = carry[2 * b], carry[2 * b + 1]
            res, ims = [], []
            for j in range(SUBLANES):
                res.append(hre)
                ims.append(him)
                sf = s_ref[pl.ds(b * n_chunk + lo + j, 1), :]
                sb = s_ref[pl.ds(b * n_chunk + hi + SUBLANES - 1 - j, 1), :]
                sre = jnp.where(is_fwd, sf[:, :S5_HALF], sb[:, :S5_HALF])
                sim = jnp.where(is_fwd, sf[:, S5_HALF:], sb[:, S5_HALF:])
                hre, him = ar * hre - ai * him + sre, ar * him + ai * hre + sim
            rows_f = pl.ds(pl.multiple_of(b * n_chunk + lo, SUBLANES), SUBLANES)
            rows_b = pl.ds(pl.multiple_of(b * n_chunk + hi, SUBLANES), SUBLANES)
            hf_ref[rows_f, 0:S5_HALF] = _rows_to_tile(res, reverse=False)
            hf_ref[rows_f, S5_HALF:] = _rows_to_tile(ims, reverse=False)
            hb_ref[rows_b, 0:S5_HALF] = _rows_to_tile(res, reverse=True)
            hb_ref[rows_b, S5_HALF:] = _rows_to_tile(ims, reverse=True)
            out += [hre, him]
        return tuple(out)

    init = tuple(jnp.zeros((1, S5_HALF), F32) for _ in range(2 * n_seq))
    lax.fori_loop(0, n_chunk // SUBLANES, step, init)

    lane = lax.broadcasted_iota(jnp.int32, (1, 2 * S5_HALF), 1)
    fwd_lane = (lane % S5_HALF) < S5_STATE
    h = jnp.where(fwd_lane, hf_ref[...], hb_ref[...]).astype(BF16)
    y_ref[...] = (jnp.dot(u, m_ref[...], preferred_element_type=F32)
                  + jnp.dot(h, v_ref[...], preferred_element_type=F32))


def _s5_scan(u, w, m, v, a_t, n_seq):
    groups, rows, vec = u.shape
    n_chunk = rows // n_seq
    return pl.pallas_call(
        functools.partial(_s5_kernel, n_seq=n_seq, n_chunk=n_chunk),
        grid=(groups,),
        in_specs=[
            pl.BlockSpec((None, rows, vec), lambda g: (g, 0, 0)),
            pl.BlockSpec((None, vec, 2 * S5_HALF), lambda g: (g, 0, 0)),
            pl.BlockSpec((None, vec, vec), lambda g: (g, 0, 0)),
            pl.BlockSpec((None, 2 * S5_HALF, vec), lambda g: (g, 0, 0)),
            pl.BlockSpec((None, 2, S5_HALF), lambda g: (g, 0, 0)),
        ],
        out_specs=pl.BlockSpec((None, rows, vec), lambda g: (g, 0, 0)),
        out_shape=jax.ShapeDtypeStruct((groups, rows, vec), F32),
        scratch_shapes=[
            pltpu.VMEM((rows, 2 * S5_HALF), F32),
            pltpu.VMEM((rows, 2 * S5_HALF), F32),
            pltpu.VMEM((rows, 2 * S5_HALF), F32),
        ],
        compiler_params=_params("arbitrary"),
        name="s5_scan",
    )(u, w, m, v, a_t)


def _s5_operators(lam_re, lam_im, log_dt, b_re, b_im, c_re, c_im, d_skip):
    hi = lax.Precision.HIGHEST
    t = S5_CHUNK
    groups = lam_re.shape[1]
    lr = jnp.minimum(lam_re.astype(F32), -1e-4)
    li = lam_im.astype(F32)
    dt = jnp.exp(log_dt.astype(F32))[..., None]
    mag = jnp.exp(lr * dt)
    ab_re = mag * jnp.cos(li * dt)
    ab_im = mag * jnp.sin(li * dt)
    nr, ni = ab_re - 1.0, ab_im
    den = lr * lr + li * li
    f_re = (nr * lr + ni * li) / den
    f_im = (ni * lr - nr * li) / den
    br, bi = b_re.astype(F32), b_im.astype(F32)
    bb_re = f_re[..., None] * br - f_im[..., None] * bi
    bb_im = f_re[..., None] * bi + f_im[..., None] * br
    cr, ci = c_re.astype(F32), c_im.astype(F32)

    k = jnp.arange(t + 1, dtype=F32)
    mag_k = jnp.exp((lr * dt)[..., None] * k)
    ang_k = (li * dt)[..., None] * k
    pr = mag_k * jnp.cos(ang_k)
    pi = mag_k * jnp.sin(ang_k)

    ab_r = pr[..., :t, None] * bb_re[..., None, :] - pi[..., :t, None] * bb_im[..., None, :]
    ab_i = pr[..., :t, None] * bb_im[..., None, :] + pi[..., :t, None] * bb_re[..., None, :]
    lag = (jnp.einsum('zgnp,zgpkm->zgknm', cr, ab_r, precision=hi)
           - jnp.einsum('zgnp,zgpkm->zgknm', ci, ab_i, precision=hi))

    s_idx = jnp.arange(t)[:, None]
    t_idx = jnp.arange(t)[None, :]
    diff = t_idx - s_idx
    fwd = jnp.where((diff >= 0)[None, :, :, None, None], lag[0][:, jnp.clip(diff, 0, t - 1)], 0.0)
    bwd = jnp.where((diff <= 0)[None, :, :, None, None], lag[1][:, jnp.clip(-diff, 0, t - 1)], 0.0)
    skip = (jnp.eye(t, dtype=F32)[None, :, :, None, None]
            * (d_skip.astype(F32).reshape(groups, S5_GROUP)[:, None, None, :, None]
               * jnp.eye(S5_GROUP, dtype=F32)[None, None, None]))
    m_op = (fwd + bwd + skip).transpose(0, 1, 4, 2, 3).reshape(groups, S5_VEC, S5_VEC)

    pf_r, pf_i = pr[0][..., :t][..., ::-1], pi[0][..., :t][..., ::-1]
    pb_r, pb_i = pr[1][..., :t], pi[1][..., :t]

    def contrib(p_r, p_i, z):
        re = p_r[..., None] * bb_re[z][:, :, None, :] - p_i[..., None] * bb_im[z][:, :, None, :]
        im = p_r[..., None] * bb_im[z][:, :, None, :] + p_i[..., None] * bb_re[z][:, :, None, :]
        to_rows = lambda x: x.transpose(0, 2, 3, 1).reshape(groups, S5_VEC, S5_STATE)
        return to_rows(re), to_rows(im)

    wf_re, wf_im = contrib(pf_r, pf_i, 0)
    wb_re, wb_im = contrib(pb_r, pb_i, 1)
    w_op = jnp.concatenate([wf_re, wb_re, wf_im, wb_im], axis=-1)

    qf_r, qf_i = pr[0][..., 1:], pi[0][..., 1:]
    qb_r, qb_i = pr[1][..., 1:][..., ::-1], pi[1][..., 1:][..., ::-1]

    def readout(q_r, q_i, z):
        c_r = cr[z].transpose(0, 2, 1)[:, :, None, :]
        c_i = ci[z].transpose(0, 2, 1)[:, :, None, :]
        w_r = c_r * q_r[..., None] - c_i * q_i[..., None]
        w_i = c_r * q_i[..., None] + c_i * q_r[..., None]
        return w_r.reshape(groups, S5_STATE, S5_VEC), (-w_i).reshape(groups, S5_STATE, S5_VEC)

    vf_re, vf_im = readout(qf_r, qf_i, 0)
    vb_re, vb_im = readout(qb_r, qb_i, 1)
    v_op = jnp.concatenate([vf_re, vb_re, vf_im, vb_im], axis=1)

    a_t = jnp.stack([jnp.concatenate([pr[0][..., t], pr[1][..., t]], axis=-1),
                     jnp.concatenate([pi[0][..., t], pi[1][..., t]], axis=-1)], axis=1)
    return w_op.astype(BF16), m_op.astype(BF16), v_op.astype(BF16), a_t


def _rope(x, cos, sin_lo, sin_hi):
    return (x * cos
            + pltpu.roll(x, HEAD_DIM - ROPE_HALF, 1) * sin_lo
            + pltpu.roll(x, ROPE_HALF, 1) * sin_hi)


def _attn_kernel(sink_ref, q_ref, k_ref, v_ref, qn_ref, kn_ref, cos_ref, slo_ref, shi_ref,
                 o_ref, kb_ref, vb_ref, *, seq):
    h = pl.program_id(1)
    n = pl.program_id(2)
    prep_rows = 256

    @pl.when(n == 0)
    def _():
        pad = jnp.zeros((BLOCK, HEAD_DIM), BF16)
        kb_ref[0:BLOCK, :] = pad
        vb_ref[0:BLOCK, :] = pad
        kb_ref[BLOCK + seq:, :] = pad
        vb_ref[BLOCK + seq:, :] = pad

        def prep(c, carry):
            r0 = pl.multiple_of(c * prep_rows, prep_rows)
            rows = pl.ds(r0, prep_rows)
            k = _rms_normalize(k_ref[rows, :], kn_ref[...])
            k = _rope(k, cos_ref[rows, :], slo_ref[rows, :], shi_ref[rows, :])
            kb_ref[pl.ds(r0 + BLOCK, prep_rows), :] = k.astype(BF16)
            vb_ref[pl.ds(r0 + BLOCK, prep_rows), :] = v_ref[rows, :].astype(BF16)
            return carry

        lax.fori_loop(0, seq // prep_rows, prep, 0)

    r0 = pl.multiple_of(n * BLOCK, BLOCK)
    rows = pl.ds(r0, BLOCK)
    cos, slo, shi = cos_ref[rows, :], slo_ref[rows, :], shi_ref[rows, :]
    heads = []
    for g in range(Q_PER_KV):
        qg = _rms_normalize(q_ref[:, g * HEAD_DIM:(g + 1) * HEAD_DIM], qn_ref[...])
        heads.append(_rope(qg, cos, slo, shi).astype(BF16))
    q4 = jnp.concatenate(heads, axis=0)
    kw = kb_ref[pl.ds(r0, 3 * BLOCK), :]
    vw = vb_ref[pl.ds(r0, 3 * BLOCK), :]
    s = lax.dot_general(q4, kw, (((1,), (1,)), ((), ())), preferred_element_type=F32)
    s = s * (HEAD_DIM ** -0.5)

    qi = lax.broadcasted_iota(jnp.int32, (BLOCK, 1), 0)
    kj = lax.broadcasted_iota(jnp.int32, (BLOCK, 3 * BLOCK), 1)
    lo = jnp.maximum(qi + (BLOCK - WINDOW), BLOCK - r0)
    hi = jnp.minimum(qi + (BLOCK + WINDOW), seq - 1 + BLOCK - r0)
    probs = []
    inv_denoms = []
    for g in range(Q_PER_KV):
        sg = s[g * BLOCK:(g + 1) * BLOCK, :]
        sg = jnp.where(kj >= lo, jnp.where(kj <= hi, sg, NEG_INF), NEG_INF)
        sk = sink_ref[h * Q_PER_KV + g]
        mx = jnp.maximum(jnp.max(sg, axis=-1, keepdims=True), sk)
        p = jnp.exp(sg - mx)
        denom = jnp.sum(p, axis=-1, keepdims=True) + jnp.exp(sk - mx)
        probs.append(p.astype(BF16))
        inv_denoms.append(1.0 / denom)
    p4 = jnp.concatenate(probs, axis=0)
    o4 = jnp.dot(p4, vw, preferred_element_type=F32)
    for g in range(Q_PER_KV):
        og = o4[g * BLOCK:(g + 1) * BLOCK, :] * inv_denoms[g]
        o_ref[:, g * HEAD_DIM:(g + 1) * HEAD_DIM] = og.astype(o_ref.dtype)


def _attention(proj, sink, q_norm, k_norm, cos, sin_lo, sin_hi, q_col0, k_col0, v_col0):
    bsz, seq, _ = proj.shape
    q_w = Q_PER_KV * HEAD_DIM
    q_blk0, k_blk0, v_blk0 = q_col0 // q_w, k_col0 // HEAD_DIM, v_col0 // HEAD_DIM
    table = pl.BlockSpec((seq, HEAD_DIM), lambda b, h, n: (0, 0))
    return pl.pallas_call(
        functools.partial(_attn_kernel, seq=seq),
        grid=(bsz, N_KV_HEADS, seq // BLOCK),
        in_specs=[
            pl.BlockSpec(memory_space=pltpu.SMEM),
            pl.BlockSpec((None, BLOCK, q_w), lambda b, h, n: (b, n, q_blk0 + h)),
            pl.BlockSpec((None, seq, HEAD_DIM), lambda b, h, n: (b, 0, k_blk0 + h)),
            pl.BlockSpec((None, seq, HEAD_DIM), lambda b, h, n: (b, 0, v_blk0 + h)),
            pl.BlockSpec((1, HEAD_DIM), lambda b, h, n: (0, 0)),
            pl.BlockSpec((1, HEAD_DIM), lambda b, h, n: (0, 0)),
            table, table, table,
        ],
        out_specs=pl.BlockSpec((None, BLOCK, q_w), lambda b, h, n: (b, n, h)),
        out_shape=jax.ShapeDtypeStruct((bsz, seq, N_KV_HEADS * q_w), BF16),
        scratch_shapes=[
            pltpu.VMEM((seq + 2 * BLOCK, HEAD_DIM), BF16),
            pltpu.VMEM((seq + 2 * BLOCK, HEAD_DIM), BF16),
        ],
        compiler_params=_params("arbitrary", "arbitrary", "arbitrary"),
        name="window_attention",
    )(sink, proj, proj, proj, q_norm.reshape(1, HEAD_DIM), k_norm.reshape(1, HEAD_DIM), cos, sin_lo, sin_hi)


def _rope_tables(seq):
    inv = ROPE_THETA ** (-jnp.arange(ROPE_HALF, dtype=F32) / ROPE_HALF)
    ang = jnp.arange(seq).astype(F32)[:, None] * inv[None, :]
    cos, sin = jnp.cos(ang), jnp.sin(ang)
    rest = HEAD_DIM - 2 * ROPE_HALF
    cos_t = jnp.concatenate([cos, cos, jnp.ones((seq, rest), F32)], axis=-1)
    sin_lo = jnp.concatenate([-sin, jnp.zeros((seq, HEAD_DIM - ROPE_HALF), F32)], axis=-1)
    sin_hi = jnp.concatenate([jnp.zeros((seq, ROPE_HALF), F32), sin, jnp.zeros((seq, rest), F32)], axis=-1)
    return cos_t, sin_lo, sin_hi


def _rglru_kernel(gate_ref, xb_ref, cw_ref, cb_ref, wf_ref, wb_ref, bias_ref, lam_ref, o_ref,
                  xpad_ref, hf_ref, hb_ref, af_ref, bf_ref, ab_ref, bb_ref, *, seq, tc):
    halo = 8
    n_chunk = seq // tc
    zero_halo = jnp.zeros((halo, LRU_BS), F32)
    xpad_ref[0:halo, :] = zero_halo
    xpad_ref[halo + seq:, :] = zero_halo

    def copy_in(c, carry):
        r0 = pl.multiple_of(c * tc, tc)
        xpad_ref[pl.ds(r0 + halo, tc), :] = xb_ref[pl.ds(r0, tc), :]
        return carry

    lax.fori_loop(0, n_chunk, copy_in, 0)

    cw = cw_ref[...]
    cb = cb_ref[...]

    def conv_chunk(r0):
        ext = xpad_ref[pl.ds(r0, tc + 2 * halo), :]
        acc = cb + cw[CONV_LEFT:CONV_LEFT + 1, :] * ext[halo:halo + tc, :]
        for tap in range(CONV_WIDTH):
            if tap == CONV_LEFT:
                continue
            shifted = pltpu.roll(ext, (CONV_LEFT - tap) % (tc + 2 * halo), 0)
            acc = acc + cw[tap:tap + 1, :] * shifted[halo:halo + tc, :]
        return acc

    def softplus(x):
        return jnp.maximum(x, 0.0) + jnp.log1p(jnp.exp(-jnp.abs(x)))

    def gates(conv, w_ref, z, a_out, b_out):
        pre = jnp.dot(conv.astype(BF16), w_ref[...], preferred_element_type=F32)
        r = jax.nn.sigmoid(pre[:, :LRU_BS] + bias_ref[2 * z:2 * z + 1, :])
        i = jax.nn.sigmoid(pre[:, LRU_BS:] + bias_ref[2 * z + 1:2 * z + 2, :])
        log_a = -LRU_C * r * softplus(-lam_ref[z:z + 1, :])
        a = jnp.exp(log_a)
        a_out[...] = a
        b_out[...] = jnp.sqrt(-jnp.tanh(log_a) * (a * a + 1.0)) * (i * conv)

    def chunk_pair(c, carry):
        rf0 = pl.multiple_of(c * tc, tc)
        rb0 = pl.multiple_of((n_chunk - 1 - c) * tc, tc)
        gates(conv_chunk(rf0), wf_ref, 0, af_ref, bf_ref)
        gates(conv_chunk(rb0), wb_ref, 1, ab_ref, bb_ref)

        def rows(k, hs):
            hf, hb = hs
            lo = pl.multiple_of(k * SUBLANES, SUBLANES)
            hi = pl.multiple_of(tc - SUBLANES - k * SUBLANES, SUBLANES)
            fs, bs = [], []
            for j in range(SUBLANES):
                hf = af_ref[pl.ds(lo + j, 1), :] * hf + bf_ref[pl.ds(lo + j, 1), :]
                rr = hi + SUBLANES - 1 - j
                hb = ab_ref[pl.ds(rr, 1), :] * hb + bb_ref[pl.ds(rr, 1), :]
                fs.append(hf)
                bs.append(hb)
            hf_ref[pl.ds(pl.multiple_of(rf0 + lo, SUBLANES), SUBLANES), :] = _rows_to_tile(fs, reverse=False)
            hb_ref[pl.ds(pl.multiple_of(rb0 + hi, SUBLANES), SUBLANES), :] = _rows_to_tile(bs, reverse=True)
            return hf, hb

        return lax.fori_loop(0, tc // SUBLANES, rows, carry)

    h0 = jnp.zeros((1, LRU_BS), F32)
    lax.fori_loop(0, n_chunk, chunk_pair, (h0, h0))

    def finish(c, carry):
        rows = pl.ds(pl.multiple_of(c * tc, tc), tc)
        y = (hf_ref[rows, :] + hb_ref[rows, :]) * jax.nn.gelu(gate_ref[rows, :])
        o_ref[rows, :] = y.astype(o_ref.dtype)
        return carry

    lax.fori_loop(0, n_chunk, finish, 0)


def _rglru(proj, conv_w, conv_b, w_fwd, w_bwd, bias, lam, tc):
    bsz, seq, width2 = proj.shape
    n_blk = width2 // (2 * LRU_BS)
    seq_blk = lambda off: pl.BlockSpec((None, seq, LRU_BS), lambda b, n: (b, 0, off + n))
    per_blk = lambda rows, cols: pl.BlockSpec((None, rows, cols), lambda b, n: (n, 0, 0))
    return pl.pallas_call(
        functools.partial(_rglru_kernel, seq=seq, tc=tc),
        grid=(bsz, n_blk),
        in_specs=[
            seq_blk(0),
            seq_blk(n_blk),
            pl.BlockSpec((CONV_WIDTH, LRU_BS), lambda b, n: (0, n)),
            pl.BlockSpec((1, LRU_BS), lambda b, n: (0, n)),
            per_blk(LRU_BS, 2 * LRU_BS),
            per_blk(LRU_BS, 2 * LRU_BS),
            per_blk(4, LRU_BS),
            per_blk(2, LRU_BS),
        ],
        out_specs=pl.BlockSpec((None, seq, LRU_BS), lambda b, n: (b, 0, n)),
        out_shape=jax.ShapeDtypeStruct((bsz, seq, n_blk * LRU_BS), BF16),
        scratch_shapes=[
            pltpu.VMEM((seq + 16, LRU_BS), F32),
            pltpu.VMEM((seq, LRU_BS), F32),
            pltpu.VMEM((seq, LRU_BS), F32),
            pltpu.VMEM((tc, LRU_BS), F32),
            pltpu.VMEM((tc, LRU_BS), F32),
            pltpu.VMEM((tc, LRU_BS), F32),
            pltpu.VMEM((tc, LRU_BS), F32),
        ],
        compiler_params=_params("arbitrary", "arbitrary"),
        name="rglru",
    )(proj, proj, conv_w, conv_b.reshape(1, -1), w_fwd, w_bwd, bias, lam)


def _even_layer(x, p, bsz, seq):
    m, d = x.shape
    s5_width = p["s5_w_glu"].shape[0]
    attn_width = N_KV_HEADS * Q_PER_KV * HEAD_DIM
    kv_width = N_KV_HEADS * HEAD_DIM
    proj = _norm_matmul(x, p["norm_mix"], p["w_in"], tm=1024, tn=512)

    groups = s5_width // S5_GROUP
    n_chunk_rows = m // S5_CHUNK
    u = proj[:, :s5_width].reshape(n_chunk_rows, S5_CHUNK, groups, S5_GROUP)
    u = u.transpose(2, 0, 1, 3).reshape(groups, n_chunk_rows, S5_VEC).astype(BF16)
    y = _s5_scan(u, p["s5_w"], p["s5_m"], p["s5_v"], p["s5_a"], n_seq=bsz)
    y = y.reshape(groups, n_chunk_rows, S5_CHUNK, S5_GROUP).transpose(1, 2, 0, 3).reshape(m, s5_width)
    y_s5 = _glu(y, p["s5_w_glu"], p["s5_b_glu"], tm=512)

    y_attn = _attention(proj.reshape(bsz, seq, -1), p["attn_sink"], p["attn_q_norm"], p["attn_k_norm"],
                        p["rope_cos"], p["rope_sin_lo"], p["rope_sin_hi"],
                        q_col0=s5_width, k_col0=s5_width + attn_width,
                        v_col0=s5_width + attn_width + kv_width)
    return _proj_residual(x, [y_s5, y_attn.reshape(m, attn_width)],
                          [p["w_out"][:s5_width], p["w_out"][s5_width:]], tm=1024, tn=512)


def _odd_layer(x, p, bsz, seq):
    m, d = x.shape
    proj = _norm_matmul(x, p["norm_mix"], p["w_in"], tm=1024, tn=512)
    y = _rglru(proj.reshape(bsz, seq, -1), p["conv_w"], p["conv_b"], p["w_fwd"], p["w_bwd"],
               p["gate_bias"], p["lam"], tc=256)
    return _proj_residual(x, [y.reshape(m, -1)], [p["w_out"]], tm=1024, tn=512)


def kernel(x_prompt, x_sample, norm_mix, norm_ffn, ev_w_in, ev_w_out, s5_lam_re, s5_lam_im, s5_log_dt, s5_b_re, s5_b_im, s5_c_re, s5_c_im, s5_d, s5_w_glu, s5_b_glu, attn_q_norm, attn_k_norm, attn_sink, od_w_in, od_w_out, lru_conv_w, lru_conv_b, lru_wa, lru_ba, lru_wx, lru_bx, lru_lam, ffn_w1, ffn_w3, ffn_w2):
    depth = norm_mix.shape[0]
    seq_lens = {x_prompt.shape[1], x_sample.shape[1]}
    rope = {s: _rope_tables(s) for s in seq_lens}

    layers = []
    for layer in range(depth):
        if layer % 2 == 0:
            e = layer // 2
            s5_w, s5_m, s5_v, s5_a = _s5_operators(s5_lam_re[e], s5_lam_im[e], s5_log_dt[e], s5_b_re[e],
                                                   s5_b_im[e], s5_c_re[e], s5_c_im[e], s5_d[e])
            mix = dict(norm_mix=norm_mix[layer], w_in=ev_w_in[e].astype(BF16), w_out=ev_w_out[e].astype(BF16),
                       s5_w=s5_w, s5_m=s5_m, s5_v=s5_v, s5_a=s5_a,
                       s5_w_glu=s5_w_glu[e].astype(BF16), s5_b_glu=s5_b_glu[e],
                       attn_q_norm=attn_q_norm[e], attn_k_norm=attn_k_norm[e], attn_sink=attn_sink[e])
        else:
            o = layer // 2
            n_blk = lru_wa.shape[2]
            mix = dict(norm_mix=norm_mix[layer], w_in=od_w_in[o].astype(BF16), w_out=od_w_out[o].astype(BF16),
                       conv_w=lru_conv_w[o], conv_b=lru_conv_b[o],
                       w_fwd=jnp.concatenate([lru_wa[o, 0], lru_wx[o, 0]], axis=-1).astype(BF16),
                       w_bwd=jnp.concatenate([lru_wa[o, 1], lru_wx[o, 1]], axis=-1).astype(BF16),
                       gate_bias=jnp.stack([lru_ba[o, 0], lru_bx[o, 0], lru_ba[o, 1], lru_bx[o, 1]], axis=0)
                       .reshape(4, n_blk, LRU_BS).transpose(1, 0, 2),
                       lam=lru_lam[o].reshape(2, n_blk, LRU_BS).transpose(1, 0, 2))
        ffn = dict(gain=norm_ffn[layer], w1=ffn_w1[layer].astype(BF16), w3=ffn_w3[layer].astype(BF16),
                   w2=ffn_w2[layer].astype(BF16))
        layers.append((mix, ffn))

    def run(x):
        bsz, seq, d = x.shape
        h = x.reshape(bsz * seq, d)
        for layer, (mix, ffn) in enumerate(layers):
            if layer % 2 == 0:
                cos, sin_lo, sin_hi = rope[seq]
                h = _even_layer(h, dict(mix, rope_cos=cos, rope_sin_lo=sin_lo, rope_sin_hi=sin_hi), bsz, seq)
            else:
                h = _odd_layer(h, mix, bsz, seq)
            h = _ffn(h, ffn["gain"], ffn["w1"], ffn["w3"], ffn["w2"], tm=512, tf=512)
        return h.reshape(bsz, seq, d)

    return (run(x_prompt), run(x_sample))
```

```python
import functools
import math

import jax
import jax.numpy as jnp
from jax import lax
from jax.experimental import pallas as pl
from jax.experimental.pallas import tpu as pltpu

F32 = jnp.float32
BF16 = jnp.bfloat16

LANES = 128
SUBLANES = 8
VMEM_LIMIT_BYTES = 56 * 1024 * 1024

EPS = 1e-6
NEG_INF = -1e30
S5_GROUP = 16
S5_STATE = 64
HEAD_DIM = 128
N_KV_HEADS = 2
Q_PER_KV = 4
WINDOW = 128
BLOCK = 128
ROPE_HALF = 16
ROPE_THETA = 500000.0
LRU_BS = 256
LRU_C = 8.0
CONV_WIDTH = 4
CONV_LEFT = 2

S5_CHUNK = 16
S5_VEC = S5_CHUNK * S5_GROUP
S5_HALF = 2 * S5_STATE


def _params(*semantics):
    return pltpu.CompilerParams(dimension_semantics=semantics, vmem_limit_bytes=VMEM_LIMIT_BYTES)


def _rows_to_tile(rows, reverse):
    n = rows[0].shape[1]
    sub = lax.broadcasted_iota(jnp.int32, (SUBLANES, n), 0)
    tile = jnp.broadcast_to(rows[0], (SUBLANES, n))
    for j in range(1, SUBLANES):
        tile = jnp.where(sub == (SUBLANES - 1 - j if reverse else j), jnp.broadcast_to(rows[j], (SUBLANES, n)), tile)
    return tile


def _rms_normalize(x, gain):
    ms = jnp.mean(x * x, axis=-1, keepdims=True)
    return x * lax.rsqrt(ms + EPS) * gain


def _norm_matmul_kernel(x_ref, g_ref, w_ref, o_ref, hn_ref):
    @pl.when(pl.program_id(1) == 0)
    def _():
        hn_ref[...] = _rms_normalize(x_ref[...], g_ref[...]).astype(BF16)

    o_ref[...] = jnp.dot(hn_ref[...], w_ref[...], preferred_element_type=F32)


def _norm_matmul(x, gain, w, tm, tn):
    m, d = x.shape
    n = w.shape[1]
    return pl.pallas_call(
        _norm_matmul_kernel,
        grid=(m // tm, n // tn),
        in_specs=[
            pl.BlockSpec((tm, d), lambda i, j: (i, 0)),
            pl.BlockSpec((1, d), lambda i, j: (0, 0)),
            pl.BlockSpec((d, tn), lambda i, j: (0, j)),
        ],
        out_specs=pl.BlockSpec((tm, tn), lambda i, j: (i, j)),
        out_shape=jax.ShapeDtypeStruct((m, n), F32),
        scratch_shapes=[pltpu.VMEM((tm, d), BF16)],
        compiler_params=_params("arbitrary", "arbitrary"),
        name="norm_matmul",
    )(x, gain.reshape(1, d), w)


def _proj_residual_kernel(*refs, n_in):
    x_ref = refs[0]
    a_refs = refs[1:1 + n_in]
    w_refs = refs[1 + n_in:1 + 2 * n_in]
    o_ref = refs[1 + 2 * n_in]
    acc = x_ref[...]
    for a_ref, w_ref in zip(a_refs, w_refs):
        acc = acc + jnp.dot(a_ref[...], w_ref[...], preferred_element_type=F32)
    o_ref[...] = acc


def _proj_residual(x, acts, weights, tm, tn):
    m, n = x.shape
    n_in = len(acts)
    in_specs = [pl.BlockSpec((tm, tn), lambda i, j: (i, j))]
    in_specs += [pl.BlockSpec((tm, a.shape[1]), lambda i, j: (i, 0)) for a in acts]
    in_specs += [pl.BlockSpec((w.shape[0], tn), lambda i, j: (0, j)) for w in weights]
    return pl.pallas_call(
        functools.partial(_proj_residual_kernel, n_in=n_in),
        grid=(m // tm, n // tn),
        in_specs=in_specs,
        out_specs=pl.BlockSpec((tm, tn), lambda i, j: (i, j)),
        out_shape=jax.ShapeDtypeStruct((m, n), F32),
        compiler_params=_params("arbitrary", "arbitrary"),
        name="proj_residual",
    )(x, *acts, *weights)


def _ffn_kernel(x_ref, g_ref, w1_ref, w3_ref, w2_ref, o_ref, hn_ref):
    j = pl.program_id(1)

    @pl.when(j == 0)
    def _():
        x = x_ref[...]
        hn_ref[...] = _rms_normalize(x, g_ref[...]).astype(BF16)
        o_ref[...] = x

    hn = hn_ref[...]
    a = jnp.dot(hn, w1_ref[...], preferred_element_type=F32)
    b = jnp.dot(hn, w3_ref[...], preferred_element_type=F32)
    act = (a * jax.nn.sigmoid(a) * b).astype(BF16)
    o_ref[...] += jnp.dot(act, w2_ref[...], preferred_element_type=F32)


def _ffn(x, gain, w1, w3, w2, tm, tf):
    m, d = x.shape
    f = w1.shape[1]
    return pl.pallas_call(
        _ffn_kernel,
        grid=(m // tm, f // tf),
        in_specs=[
            pl.BlockSpec((tm, d), lambda i, j: (i, 0)),
            pl.BlockSpec((1, d), lambda i, j: (0, 0)),
            pl.BlockSpec((d, tf), lambda i, j: (0, j)),
            pl.BlockSpec((d, tf), lambda i, j: (0, j)),
            pl.BlockSpec((tf, d), lambda i, j: (j, 0)),
        ],
        out_specs=pl.BlockSpec((tm, d), lambda i, j: (i, 0)),
        out_shape=jax.ShapeDtypeStruct((m, d), F32),
        scratch_shapes=[pltpu.VMEM((tm, d), BF16)],
        compiler_params=_params("arbitrary", "arbitrary"),
        name="ffn",
    )(x, gain.reshape(1, d), w1, w3, w2)


def _glu_kernel(y_ref, w_ref, b_ref, o_ref):
    g = jax.nn.gelu(y_ref[...])
    z = jnp.dot(g.astype(BF16), w_ref[...], preferred_element_type=F32) + b_ref[...]
    o_ref[...] = (g * jax.nn.sigmoid(z)).astype(o_ref.dtype)


def _glu(y, w, b, tm):
    m, c = y.shape
    return pl.pallas_call(
        _glu_kernel,
        grid=(m // tm,),
        in_specs=[
            pl.BlockSpec((tm, c), lambda i: (i, 0)),
            pl.BlockSpec((c, c), lambda i: (0, 0)),
            pl.BlockSpec((1, c), lambda i: (0, 0)),
        ],
        out_specs=pl.BlockSpec((tm, c), lambda i: (i, 0)),
        out_shape=jax.ShapeDtypeStruct((m, c), BF16),
        compiler_params=_params("arbitrary"),
        name="s5_glu",
    )(y, w, b.reshape(1, c))


def _s5_kernel(u_ref, w_ref, m_ref, v_ref, a_ref, y_ref, s_ref, hf_ref, hb_ref, *, n_seq, n_chunk):
    u = u_ref[...]
    s_ref[...] = jnp.dot(u, w_ref[...], preferred_element_type=F32)
    ar = a_ref[0:1, :]
    ai = a_ref[1:2, :]
    is_fwd = lax.broadcasted_iota(jnp.int32, (1, S5_HALF), 1) < S5_STATE

    def step(k, carry):
        lo = pl.multiple_of(k * SUBLANES, SUBLANES)
        hi = pl.multiple_of(n_chunk - SUBLANES - k * SUBLANES, SUBLANES)
        out = []
        for b in range(n_seq):
            hre, him = carry[2 * b], carry[2 * b + 1]
            res, ims = [], []
            for j in range(SUBLANES):
                res.append(hre)
                ims.append(him)
                sf = s_ref[pl.ds(b * n_chunk + lo + j, 1), :]
                sb = s_ref[pl.ds(b * n_chunk + hi + SUBLANES - 1 - j, 1), :]
                sre = jnp.where(is_fwd, sf[:, :S5_HALF], sb[:, :S5_HALF])
                sim = jnp.where(is_fwd, sf[:, S5_HALF:], sb[:, S5_HALF:])
                hre, him = ar * hre - ai * him + sre, ar * him + ai * hre + sim
            rows_f = pl.ds(pl.multiple_of(b * n_chunk + lo, SUBLANES), SUBLANES)
            rows_b = pl.ds(pl.multiple_of(b * n_chunk + hi, SUBLANES), SUBLANES)
            hf_ref[rows_f, 0:S5_HALF] = _rows_to_tile(res, reverse=False)
            hf_ref[rows_f, S5_HALF:] = _rows_to_tile(ims, reverse=False)
            hb_ref[rows_b, 0:S5_HALF] = _rows_to_tile(res, reverse=True)
            hb_ref[rows_b, S5_HALF:] = _rows_to_tile(ims, reverse=True)
            out += [hre, him]
        return tuple(out)

    init = tuple(jnp.zeros((1, S5_HALF), F32) for _ in range(2 * n_seq))
    lax.fori_loop(0, n_chunk // SUBLANES, step, init)

    lane = lax.broadcasted_iota(jnp.int32, (1, 2 * S5_HALF), 1)
    fwd_lane = (lane % S5_HALF) < S5_STATE
    h = jnp.where(fwd_lane, hf_ref[...], hb_ref[...]).astype(BF16)
    y_ref[...] = (jnp.dot(u, m_ref[...], preferred_element_type=F32)
                  + jnp.dot(h, v_ref[...], preferred_element_type=F32))


def _s5_scan(u, w, m, v, a_t, n_seq):
    groups, rows, vec = u.shape
    n_chunk = rows // n_seq
    return pl.pallas_call(
        functools.partial(_s5_kernel, n_seq=n_seq, n_chunk=n_chunk),
        grid=(groups,),
        in_specs=[
            pl.BlockSpec((None, rows, vec), lambda g: (g, 0, 0)),
            pl.BlockSpec((None, vec, 2 * S5_HALF), lambda g: (g, 0, 0)),
            pl.BlockSpec((None, vec, vec), lambda g: (g, 0, 0)),
            pl.BlockSpec((None, 2 * S5_HALF, vec), lambda g: (g, 0, 0)),
            pl.BlockSpec((None, 2, S5_HALF), lambda g: (g, 0, 0)),
        ],
        out_specs=pl.BlockSpec((None, rows, vec), lambda g: (g, 0, 0)),
        out_shape=jax.ShapeDtypeStruct((groups, rows, vec), F32),
        scratch_shapes=[
            pltpu.VMEM((rows, 2 * S5_HALF), F32),
            pltpu.VMEM((rows, 2 * S5_HALF), F32),
            pltpu.VMEM((rows, 2 * S5_HALF), F32),
        ],
        compiler_params=_params("arbitrary"),
        name="s5_scan",
    )(u, w, m, v, a_t)


def _s5_operators(lam_re, lam_im, log_dt, b_re, b_im, c_re, c_im, d_skip):
    hi = lax.Precision.HIGHEST
    t = S5_CHUNK
    groups = lam_re.shape[1]
    lr = jnp.minimum(lam_re.astype(F32), -1e-4)
    li = lam_im.astype(F32)
    dt = jnp.exp(log_dt.astype(F32))[..., None]
    mag = jnp.exp(lr * dt)
    ab_re = mag * jnp.cos(li * dt)
    ab_im = mag * jnp.sin(li * dt)
    nr, ni = ab_re - 1.0, ab_im
    den = lr * lr + li * li
    f_re = (nr * lr + ni * li) / den
    f_im = (ni * lr - nr * li) / den
    br, bi = b_re.astype(F32), b_im.astype(F32)
    bb_re = f_re[..., None] * br - f_im[..., None] * bi
    bb_im = f_re[..., None] * bi + f_im[..., None] * br
    cr, ci = c_re.astype(F32), c_im.astype(F32)

    k = jnp.arange(t + 1, dtype=F32)
    mag_k = jnp.exp((lr * dt)[..., None] * k)
    ang_k = (li * dt)[..., None] * k
    pr = mag_k * jnp.cos(ang_k)
    pi = mag_k * jnp.sin(ang_k)

    ab_r = pr[..., :t, None] * bb_re[..., None, :] - pi[..., :t, None] * bb_im[..., None, :]
    ab_i = pr[..., :t, None] * bb_im[..., None, :] + pi[..., :t, None] * bb_re[..., None, :]
    lag = (jnp.einsum('zgnp,zgpkm->zgknm', cr, ab_r, precision=hi)
           - jnp.einsum('zgnp,zgpkm->zgknm', ci, ab_i, precision=hi))

    s_idx = jnp.arange(t)[:, None]
    t_idx = jnp.arange(t)[None, :]
    diff = t_idx - s_idx
    fwd = jnp.where((diff >= 0)[None, :, :, None, None], lag[0][:, jnp.clip(diff, 0, t - 1)], 0.0)
    bwd = jnp.where((diff <= 0)[None, :, :, None, None], lag[1][:, jnp.clip(-diff, 0, t - 1)], 0.0)
    skip = (jnp.eye(t, dtype=F32)[None, :, :, None, None]
            * (d_skip.astype(F32).reshape(groups, S5_GROUP)[:, None, None, :, None]
               * jnp.eye(S5_GROUP, dtype=F32)[None, None, None]))
    m_op = (fwd + bwd + skip).transpose(0, 1, 4, 2, 3).reshape(groups, S5_VEC, S5_VEC)

    pf_r, pf_i = pr[0][..., :t][..., ::-1], pi[0][..., :t][..., ::-1]
    pb_r, pb_i = pr[1][..., :t], pi[1][..., :t]

    def contrib(p_r, p_i, z):
        re = p_r[..., None] * bb_re[z][:, :, None, :] - p_i[..., None] * bb_im[z][:, :, None, :]
        im = p_r[..., None] * bb_im[z][:, :, None, :] + p_i[..., None] * bb_re[z][:, :, None, :]
        to_rows = lambda x: x.transpose(0, 2, 3, 1).reshape(groups, S5_VEC, S5_STATE)
        return to_rows(re), to_rows(im)

    wf_re, wf_im = contrib(pf_r, pf_i, 0)
    wb_re, wb_im = contrib(pb_r, pb_i, 1)
    w_op = jnp.concatenate([wf_re, wb_re, wf_im, wb_im], axis=-1)

    qf_r, qf_i = pr[0][..., 1:], pi[0][..., 1:]
    qb_r, qb_i = pr[1][..., 1:][..., ::-1], pi[1][..., 1:][..., ::-1]

    def readout(q_r, q_i, z):
        c_r = cr[z].transpose(0, 2, 1)[:, :, None, :]
        c_i = ci[z].transpose(0, 2, 1)[:, :, None, :]
        w_r = c_r * q_r[..., None] - c_i * q_i[..., None]
        w_i = c_r * q_i[..., None] + c_i * q_r[..., None]
        return w_r.reshape(groups, S5_STATE, S5_VEC), (-w_i).reshape(groups, S5_STATE, S5_VEC)

    vf_re, vf_im = readout(qf_r, qf_i, 0)
    vb_re, vb_im = readout(qb_r, qb_i, 1)
    v_op = jnp.concatenate([vf_re, vb_re, vf_im, vb_im], axis=1)

    a_t = jnp.stack([jnp.concatenate([pr[0][..., t], pr[1][..., t]], axis=-1),
                     jnp.concatenate([pi[0][..., t], pi[1][..., t]], axis=-1)], axis=1)
    return w_op.astype(BF16), m_op.astype(BF16), v_op.astype(BF16), a_t


def _rope(x, cos, sin_lo, sin_hi):
    return (x * cos
            + pltpu.roll(x, HEAD_DIM - ROPE_HALF, 1) * sin_lo
            + pltpu.roll(x, ROPE_HALF, 1) * sin_hi)


def _attn_kernel(sink_ref, q_ref, k_ref, v_ref, qn_ref, kn_ref, cos_ref, slo_ref, shi_ref,
                 o_ref, kb_ref, vb_ref, *, seq):
    h = pl.program_id(1)
    n = pl.program_id(2)
    prep_rows = 256

    @pl.when(n == 0)
    def _():
        pad = jnp.zeros((BLOCK, HEAD_DIM), BF16)
        kb_ref[0:BLOCK, :] = pad
        vb_ref[0:BLOCK, :] = pad
        kb_ref[BLOCK + seq:, :] = pad
        vb_ref[BLOCK + seq:, :] = pad

        def prep(c, carry):
            r0 = pl.multiple_of(c * prep_rows, prep_rows)
            rows = pl.ds(r0, prep_rows)
            k = _rms_normalize(k_ref[rows, :], kn_ref[...])
            k = _rope(k, cos_ref[rows, :], slo_ref[rows, :], shi_ref[rows, :])
            kb_ref[pl.ds(r0 + BLOCK, prep_rows), :] = k.astype(BF16)
            vb_ref[pl.ds(r0 + BLOCK, prep_rows), :] = v_ref[rows, :].astype(BF16)
            return carry

        lax.fori_loop(0, seq // prep_rows, prep, 0)

    r0 = pl.multiple_of(n * BLOCK, BLOCK)
    rows = pl.ds(r0, BLOCK)
    cos, slo, shi = cos_ref[rows, :], slo_ref[rows, :], shi_ref[rows, :]
    heads = []
    for g in range(Q_PER_KV):
        qg = _rms_normalize(q_ref[:, g * HEAD_DIM:(g + 1) * HEAD_DIM], qn_ref[...])
        heads.append(_rope(qg, cos, slo, shi).astype(BF16))
    q4 = jnp.concatenate(heads, axis=0)
    kw = kb_ref[pl.ds(r0, 3 * BLOCK), :]
    vw = vb_ref[pl.ds(r0, 3 * BLOCK), :]
    s = lax.dot_general(q4, kw, (((1,), (1,)), ((), ())), preferred_element_type=F32)
    s = s * (HEAD_DIM ** -0.5)

    qi = lax.broadcasted_iota(jnp.int32, (BLOCK, 1), 0)
    kj = lax.broadcasted_iota(jnp.int32, (BLOCK, 3 * BLOCK), 1)
    lo = jnp.maximum(qi + (BLOCK - WINDOW), BLOCK - r0)
    hi = jnp.minimum(qi + (BLOCK + WINDOW), seq - 1 + BLOCK - r0)
    probs = []
    inv_denoms = []
    for g in range(Q_PER_KV):
        sg = s[g * BLOCK:(g + 1) * BLOCK, :]
        sg = jnp.where(kj >= lo, jnp.where(kj <= hi, sg, NEG_INF), NEG_INF)
        sk = sink_ref[h * Q_PER_KV + g]
        mx = jnp.maximum(jnp.max(sg, axis=-1, keepdims=True), sk)
        p = jnp.exp(sg - mx)
        denom = jnp.sum(p, axis=-1, keepdims=True) + jnp.exp(sk - mx)
        probs.append(p.astype(BF16))
        inv_denoms.append(1.0 / denom)
    p4 = jnp.concatenate(probs, axis=0)
    o4 = jnp.dot(p4, vw, preferred_element_type=F32)
    for g in range(Q_PER_KV):
        og = o4[g * BLOCK:(g + 1) * BLOCK, :] * inv_denoms[g]
        o_ref[:, g * HEAD_DIM:(g + 1) * HEAD_DIM] = og.astype(o_ref.dtype)


def _attention(proj, sink, q_norm, k_norm, cos, sin_lo, sin_hi, q_col0, k_col0, v_col0):
    bsz, seq, _ = proj.shape
    q_w = Q_PER_KV * HEAD_DIM
    q_blk0, k_blk0, v_blk0 = q_col0 // q_w, k_col0 // HEAD_DIM, v_col0 // HEAD_DIM
    table = pl.BlockSpec((seq, HEAD_DIM), lambda b, h, n: (0, 0))
    return pl.pallas_call(
        functools.partial(_attn_kernel, seq=seq),
        grid=(bsz, N_KV_HEADS, seq // BLOCK),
        in_specs=[
            pl.BlockSpec(memory_space=pltpu.SMEM),
            pl.BlockSpec((None, BLOCK, q_w), lambda b, h, n: (b, n, q_blk0 + h)),
            pl.BlockSpec((None, seq, HEAD_DIM), lambda b, h, n: (b, 0, k_blk0 + h)),
            pl.BlockSpec((None, seq, HEAD_DIM), lambda b, h, n: (b, 0, v_blk0 + h)),
            pl.BlockSpec((1, HEAD_DIM), lambda b, h, n: (0, 0)),
            pl.BlockSpec((1, HEAD_DIM), lambda b, h, n: (0, 0)),
            table, table, table,
        ],
        out_specs=pl.BlockSpec((None, BLOCK, q_w), lambda b, h, n: (b, n, h)),
        out_shape=jax.ShapeDtypeStruct((bsz, seq, N_KV_HEADS * q_w), BF16),
        scratch_shapes=[
            pltpu.VMEM((seq + 2 * BLOCK, HEAD_DIM), BF16),
            pltpu.VMEM((seq + 2 * BLOCK, HEAD_DIM), BF16),
        ],
        compiler_params=_params("arbitrary", "arbitrary", "arbitrary"),
        name="window_attention",
    )(sink, proj, proj, proj, q_norm.reshape(1, HEAD_DIM), k_norm.reshape(1, HEAD_DIM), cos, sin_lo, sin_hi)


def _rope_tables(seq):
    inv = ROPE_THETA ** (-jnp.arange(ROPE_HALF, dtype=F32) / ROPE_HALF)
    ang = jnp.arange(seq).astype(F32)[:, None] * inv[None, :]
    cos, sin = jnp.cos(ang), jnp.sin(ang)
    rest = HEAD_DIM - 2 * ROPE_HALF
    cos_t = jnp.concatenate([cos, cos, jnp.ones((seq, rest), F32)], axis=-1)
    sin_lo = jnp.concatenate([-sin, jnp.zeros((seq, HEAD_DIM - ROPE_HALF), F32)], axis=-1)
    sin_hi = jnp.concatenate([jnp.zeros((seq, ROPE_HALF), F32), sin, jnp.zeros((seq, rest), F32)], axis=-1)
    return cos_t, sin_lo, sin_hi


def _rglru_kernel(gate_ref, xb_ref, cw_ref, cb_ref, wf_ref, wb_ref, bias_ref, lam_ref, o_ref,
                  xpad_ref, hf_ref, hb_ref, af_ref, bf_ref, ab_ref, bb_ref, *, seq, tc):
    halo = 8
    n_chunk = seq // tc
    zero_halo = jnp.zeros((halo, LRU_BS), F32)
    xpad_ref[0:halo, :] = zero_halo
    xpad_ref[halo + seq:, :] = zero_halo

    def copy_in(c, carry):
        r0 = pl.multiple_of(c * tc, tc)
        xpad_ref[pl.ds(r0 + halo, tc), :] = xb_ref[pl.ds(r0, tc), :]
        return carry

    lax.fori_loop(0, n_chunk, copy_in, 0)

    cw = cw_ref[...]
    cb = cb_ref[...]

    def conv_chunk(r0):
        ext = xpad_ref[pl.ds(r0, tc + 2 * halo), :]
        acc = cb + cw[CONV_LEFT:CONV_LEFT + 1, :] * ext[halo:halo + tc, :]
        for tap in range(CONV_WIDTH):
            if tap == CONV_LEFT:
                continue
            shifted = pltpu.roll(ext, (CONV_LEFT - tap) % (tc + 2 * halo), 0)
            acc = acc + cw[tap:tap + 1, :] * shifted[halo:halo + tc, :]
        return acc

    def softplus(x):
        return jnp.maximum(x, 0.0) + jnp.log1p(jnp.exp(-jnp.abs(x)))

    def gates(conv, w_ref, z, a_out, b_out):
        pre = jnp.dot(conv.astype(BF16), w_ref[...], preferred_element_type=F32)
        r = jax.nn.sigmoid(pre[:, :LRU_BS] + bias_ref[2 * z:2 * z + 1, :])
        i = jax.nn.sigmoid(pre[:, LRU_BS:] + bias_ref[2 * z + 1:2 * z + 2, :])
        log_a = -LRU_C * r * softplus(-lam_ref[z:z + 1, :])
        a = jnp.exp(log_a)
        a_out[...] = a
        b_out[...] = jnp.sqrt(-jnp.tanh(log_a) * (a * a + 1.0)) * (i * conv)

    def chunk_pair(c, carry):
        rf0 = pl.multiple_of(c * tc, tc)
        rb0 = pl.multiple_of((n_chunk - 1 - c) * tc, tc)
        gates(conv_chunk(rf0), wf_ref, 0, af_ref, bf_ref)
        gates(conv_chunk(rb0), wb_ref, 1, ab_ref, bb_ref)

        def rows(k, hs):
            hf, hb = hs
            lo = pl.multiple_of(k * SUBLANES, SUBLANES)
            hi = pl.multiple_of(tc - SUBLANES - k * SUBLANES, SUBLANES)
            fs, bs = [], []
            for j in range(SUBLANES):
                hf = af_ref[pl.ds(lo + j, 1), :] * hf + bf_ref[pl.ds(lo + j, 1), :]
                rr = hi + SUBLANES - 1 - j
                hb = ab_ref[pl.ds(rr, 1), :] * hb + bb_ref[pl.ds(rr, 1), :]
                fs.append(hf)
                bs.append(hb)
            hf_ref[pl.ds(pl.multiple_of(rf0 + lo, SUBLANES), SUBLANES), :] = _rows_to_tile(fs, reverse=False)
            hb_ref[pl.ds(pl.multiple_of(rb0 + hi, SUBLANES), SUBLANES), :] = _rows_to_tile(bs, reverse=True)
            return hf, hb

        return lax.fori_loop(0, tc // SUBLANES, rows, carry)

    h0 = jnp.zeros((1, LRU_BS), F32)
    lax.fori_loop(0, n_chunk, chunk_pair, (h0, h0))

    def finish(c, carry):
        rows = pl.ds(pl.multiple_of(c * tc, tc), tc)
        y = (hf_ref[rows, :] + hb_ref[rows, :]) * jax.nn.gelu(gate_ref[rows, :])
        o_ref[rows, :] = y.astype(o_ref.dtype)
        return carry

    lax.fori_loop(0, n_chunk, finish, 0)


def _rglru(proj, conv_w, conv_b, w_fwd, w_bwd, bias, lam, tc):
    bsz, seq, width2 = proj.shape
    n_blk = width2 // (2 * LRU_BS)
    seq_blk = lambda off: pl.BlockSpec((None, seq, LRU_BS), lambda b, n: (b, 0, off + n))
    per_blk = lambda rows, cols: pl.BlockSpec((None, rows, cols), lambda b, n: (n, 0, 0))
    return pl.pallas_call(
        functools.partial(_rglru_kernel, seq=seq, tc=tc),
        grid=(bsz, n_blk),
        in_specs=[
            seq_blk(0),
            seq_blk(n_blk),
            pl.BlockSpec((CONV_WIDTH, LRU_BS), lambda b, n: (0, n)),
            pl.BlockSpec((1, LRU_BS), lambda b, n: (0, n)),
            per_blk(LRU_BS, 2 * LRU_BS),
            per_blk(LRU_BS, 2 * LRU_BS),
            per_blk(4, LRU_BS),
            per_blk(2, LRU_BS),
        ],
        out_specs=pl.BlockSpec((None, seq, LRU_BS), lambda b, n: (b, 0, n)),
        out_shape=jax.ShapeDtypeStruct((bsz, seq, n_blk * LRU_BS), BF16),
        scratch_shapes=[
            pltpu.VMEM((seq + 16, LRU_BS), F32),
            pltpu.VMEM((seq, LRU_BS), F32),
            pltpu.VMEM((seq, LRU_BS), F32),
            pltpu.VMEM((tc, LRU_BS), F32),
            pltpu.VMEM((tc, LRU_BS), F32),
            pltpu.VMEM((tc, LRU_BS), F32),
            pltpu.VMEM((tc, LRU_BS), F32),
        ],
        compiler_params=_params("arbitrary", "arbitrary"),
        name="rglru",
    )(proj, proj, conv_w, conv_b.reshape(1, -1), w_fwd, w_bwd, bias, lam)


def _even_layer(x, p, bsz, seq):
    m, d = x.shape
    s5_width = p["s5_w_glu"].shape[0]
    attn_width = N_KV_HEADS * Q_PER_KV * HEAD_DIM
    kv_width = N_KV_HEADS * HEAD_DIM
    proj = _norm_matmul(x, p["norm_mix"], p["w_in"], tm=1024, tn=512)

    groups = s5_width // S5_GROUP
    n_chunk_rows = m // S5_CHUNK
    u = proj[:, :s5_width].reshape(n_chunk_rows, S5_CHUNK, groups, S5_GROUP)
    u = u.transpose(2, 0, 1, 3).reshape(groups, n_chunk_rows, S5_VEC).astype(BF16)
    y = _s5_scan(u, p["s5_w"], p["s5_m"], p["s5_v"], p["s5_a"], n_seq=bsz)
    y = y.reshape(groups, n_chunk_rows, S5_CHUNK, S5_GROUP).transpose(1, 2, 0, 3).reshape(m, s5_width)
    y_s5 = _glu(y, p["s5_w_glu"], p["s5_b_glu"], tm=512)

    y_attn = _attention(proj.reshape(bsz, seq, -1), p["attn_sink"], p["attn_q_norm"], p["attn_k_norm"],
                        p["rope_cos"], p["rope_sin_lo"], p["rope_sin_hi"],
                        q_col0=s5_width, k_col0=s5_width + attn_width,
                        v_col0=s5_width + attn_width + kv_width)
    return _proj_residual(x, [y_s5, y_attn.reshape(m, attn_width)],
                          [p["w_out"][:s5_width], p["w_out"][s5_width:]], tm=1024, tn=512)


def _odd_layer(x, p, bsz, seq):
    m, d = x.shape
    proj = _norm_matmul(x, p["norm_mix"], p["w_in"], tm=1024, tn=512)
    y = _rglru(proj.reshape(bsz, seq, -1), p["conv_w"], p["conv_b"], p["w_fwd"], p["w_bwd"],
               p["gate_bias"], p["lam"], tc=256)
    return _proj_residual(x, [y.reshape(m, -1)], [p["w_out"]], tm=1024, tn=512)


def kernel(x_prompt, x_sample, norm_mix, norm_ffn, ev_w_in, ev_w_out, s5_lam_re, s5_lam_im, s5_log_dt, s5_b_re, s5_b_im, s5_c_re, s5_c_im, s5_d, s5_w_glu, s5_b_glu, attn_q_norm, attn_k_norm, attn_sink, od_w_in, od_w_out, lru_conv_w, lru_conv_b, lru_wa, lru_ba, lru_wx, lru_bx, lru_lam, ffn_w1, ffn_w3, ffn_w2):
    depth = norm_mix.shape[0]
    seq_lens = {x_prompt.shape[1], x_sample.shape[1]}
    rope = {s: _rope_tables(s) for s in seq_lens}

    layers = []
    for layer in range(depth):
        if layer % 2 == 0:
            e = layer // 2
            s5_w, s5_m, s5_v, s5_a = _s5_operators(s5_lam_re[e], s5_lam_im[e], s5_log_dt[e], s5_b_re[e],
                                                   s5_b_im[e], s5_c_re[e], s5_c_im[e], s5_d[e])
            mix = dict(norm_mix=norm_mix[layer], w_in=ev_w_in[e].astype(BF16), w_out=ev_w_out[e].astype(BF16),
                       s5_w=s5_w, s5_m=s5_m, s5_v=s5_v, s5_a=s5_a,
                       s5_w_glu=s5_w_glu[e].astype(BF16), s5_b_glu=s5_b_glu[e],
                       attn_q_norm=attn_q_norm[e], attn_k_norm=attn_k_norm[e], attn_sink=attn_sink[e])
        else:
            o = layer // 2
            n_blk = lru_wa.shape[2]
            mix = dict(norm_mix=norm_mix[layer], w_in=od_w_in[o].astype(BF16), w_out=od_w_out[o].astype(BF16),
                       conv_w=lru_conv_w[o], conv_b=lru_conv_b[o],
                       w_fwd=jnp.concatenate([lru_wa[o, 0], lru_wx[o, 0]], axis=-1).astype(BF16),
                       w_bwd=jnp.concatenate([lru_wa[o, 1], lru_wx[o, 1]], axis=-1).astype(BF16),
                       gate_bias=jnp.stack([lru_ba[o, 0], lru_bx[o, 0], lru_ba[o, 1], lru_bx[o, 1]], axis=0)
                       .reshape(4, n_blk, LRU_BS).transpose(1, 0, 2),
                       lam=lru_lam[o].reshape(2, n_blk, LRU_BS).transpose(1, 0, 2))
        ffn = dict(gain=norm_ffn[layer], w1=ffn_w1[layer].astype(BF16), w3=ffn_w3[layer].astype(BF16),
                   w2=ffn_w2[layer].astype(BF16))
        layers.append((mix, ffn))

    def run(x):
        bsz, seq, d = x.shape
        h = x.reshape(bsz * seq, d)
        for layer, (mix, ffn) in enumerate(layers):
            if layer % 2 == 0:
                cos, sin_lo, sin_hi = rope[seq]
                h = _even_layer(h, dict(mix, rope_cos=cos, rope_sin_lo=sin_lo, rope_sin_hi=sin_hi), bsz, seq)
            else:
                h = _odd_layer(h, mix, bsz, seq)
            h = _ffn(h, ffn["gain"], ffn["w1"], ffn["w3"], ffn["w2"], tm=512, tf=512)
        return h.reshape(bsz, seq, d)

    return (run(x_prompt), run(x_sample))
```

```python
import functools
import math

import jax
import jax.numpy as jnp
from jax import lax
from jax.experimental import pallas as pl
from jax.experimental.pallas import tpu as pltpu

F32 = jnp.float32
BF16 = jnp.bfloat16

LANES = 128
SUBLANES = 8
VMEM_LIMIT_BYTES = 56 * 1024 * 1024

EPS = 1e-6
NEG_INF = -1e30
S5_GROUP = 16
S5_STATE = 64
HEAD_DIM = 128
N_KV_HEADS = 2
Q_PER_KV = 4
WINDOW = 128
BLOCK = 128
ROPE_HALF = 16
ROPE_THETA = 500000.0
LRU_BS = 256
LRU_C = 8.0
CONV_WIDTH = 4
CONV_LEFT = 2

S5_CHUNK = 16
S5_VEC = S5_CHUNK * S5_GROUP
S5_HALF = 2 * S5_STATE


def _params(*semantics):
    return pltpu.CompilerParams(dimension_semantics=semantics, vmem_limit_bytes=VMEM_LIMIT_BYTES)


def _rows_to_tile(rows, reverse):
    n = rows[0].shape[1]
    sub = lax.broadcasted_iota(jnp.int32, (SUBLANES, n), 0)
    tile = jnp.broadcast_to(rows[0], (SUBLANES, n))
    for j in range(1, SUBLANES):
        tile = jnp.where(sub == (SUBLANES - 1 - j if reverse else j), jnp.broadcast_to(rows[j], (SUBLANES, n)), tile)
    return tile


def _rms_normalize(x, gain):
    ms = jnp.mean(x * x, axis=-1, keepdims=True)
    return x * lax.rsqrt(ms + EPS) * gain


def _norm_matmul_kernel(x_ref, g_ref, w_ref, o_ref, hn_ref):
    @pl.when(pl.program_id(1) == 0)
    def _():
        hn_ref[...] = _rms_normalize(x_ref[...], g_ref[...]).astype(BF16)

    o_ref[...] = jnp.dot(hn_ref[...], w_ref[...], preferred_element_type=F32)


def _norm_matmul(x, gain, w, tm, tn):
    m, d = x.shape
    n = w.shape[1]
    return pl.pallas_call(
        _norm_matmul_kernel,
        grid=(m // tm, n // tn),
        in_specs=[
            pl.BlockSpec((tm, d), lambda i, j: (i, 0)),
            pl.BlockSpec((1, d), lambda i, j: (0, 0)),
            pl.BlockSpec((d, tn), lambda i, j: (0, j)),
        ],
        out_specs=pl.BlockSpec((tm, tn), lambda i, j: (i, j)),
        out_shape=jax.ShapeDtypeStruct((m, n), F32),
        scratch_shapes=[pltpu.VMEM((tm, d), BF16)],
        compiler_params=_params("arbitrary", "arbitrary"),
        name="norm_matmul",
    )(x, gain.reshape(1, d), w)


def _proj_residual_kernel(*refs, n_in):
    x_ref = refs[0]
    a_refs = refs[1:1 + n_in]
    w_refs = refs[1 + n_in:1 + 2 * n_in]
    o_ref = refs[1 + 2 * n_in]
    acc = x_ref[...]
    for a_ref, w_ref in zip(a_refs, w_refs):
        acc = acc + jnp.dot(a_ref[...], w_ref[...], preferred_element_type=F32)
    o_ref[...] = acc


def _proj_residual(x, acts, weights, tm, tn):
    m, n = x.shape
    n_in = len(acts)
    in_specs = [pl.BlockSpec((tm, tn), lambda i, j: (i, j))]
    in_specs += [pl.BlockSpec((tm, a.shape[1]), lambda i, j: (i, 0)) for a in acts]
    in_specs += [pl.BlockSpec((w.shape[0], tn), lambda i, j: (0, j)) for w in weights]
    return pl.pallas_call(
        functools.partial(_proj_residual_kernel, n_in=n_in),
        grid=(m // tm, n // tn),
        in_specs=in_specs,
        out_specs=pl.BlockSpec((tm, tn), lambda i, j: (i, j)),
        out_shape=jax.ShapeDtypeStruct((m, n), F32),
        compiler_params=_params("arbitrary", "arbitrary"),
        name="proj_residual",
    )(x, *acts, *weights)


def _ffn_kernel(x_ref, g_ref, w1_ref, w3_ref, w2_ref, o_ref, hn_ref):
    j = pl.program_id(1)

    @pl.when(j == 0)
    def _():
        x = x_ref[...]
        hn_ref[...] = _rms_normalize(x, g_ref[...]).astype(BF16)
        o_ref[...] = x

    hn = hn_ref[...]
    a = jnp.dot(hn, w1_ref[...], preferred_element_type=F32)
    b = jnp.dot(hn, w3_ref[...], preferred_element_type=F32)
    act = (a * jax.nn.sigmoid(a) * b).astype(BF16)
    o_ref[...] += jnp.dot(act, w2_ref[...], preferred_element_type=F32)


def _ffn(x, gain, w1, w3, w2, tm, tf):
    m, d = x.shape
    f = w1.shape[1]
    return pl.pallas_call(
        _ffn_kernel,
        grid=(m // tm, f // tf),
        in_specs=[
            pl.BlockSpec((tm, d), lambda i, j: (i, 0)),
            pl.BlockSpec((1, d), lambda i, j: (0, 0)),
            pl.BlockSpec((d, tf), lambda i, j: (0, j)),
            pl.BlockSpec((d, tf), lambda i, j: (0, j)),
            pl.BlockSpec((tf, d), lambda i, j: (j, 0)),
        ],
        out_specs=pl.BlockSpec((tm, d), lambda i, j: (i, 0)),
        out_shape=jax.ShapeDtypeStruct((m, d), F32),
        scratch_shapes=[pltpu.VMEM((tm, d), BF16)],
        compiler_params=_params("arbitrary", "arbitrary"),
        name="ffn",
    )(x, gain.reshape(1, d), w1, w3, w2)


def _glu_kernel(y_ref, w_ref, b_ref, o_ref):
    g = jax.nn.gelu(y_ref[...])
    z = jnp.dot(g.astype(BF16), w_ref[...], preferred_element_type=F32) + b_ref[...]
    o_ref[...] = (g * jax.nn.sigmoid(z)).astype(o_ref.dtype)


def _glu(y, w, b, tm):
    m, c = y.shape
    return pl.pallas_call(
        _glu_kernel,
        grid=(m // tm,),
        in_specs=[
            pl.BlockSpec((tm, c), lambda i: (i, 0)),
            pl.BlockSpec((c, c), lambda i: (0, 0)),
            pl.BlockSpec((1, c), lambda i: (0, 0)),
        ],
        out_specs=pl.BlockSpec((tm, c), lambda i: (i, 0)),
        out_shape=jax.ShapeDtypeStruct((m, c), BF16),
        compiler_params=_params("arbitrary"),
        name="s5_glu",
    )(y, w, b.reshape(1, c))


def _s5_kernel(ut_ref, w_ref, m_ref, v_ref, a_ref, yt_ref, s_ref, hf_ref, hb_ref, *, n_seq, n_chunk):
    u = ut_ref[...].T.astype(BF16)
    s_ref[...] = jnp.dot(u, w_ref[...], preferred_element_type=F32)
    ar = a_ref[0:1, :]
    ai = a_ref[1:2, :]
    is_fwd = lax.broadcasted_iota(jnp.int32, (1, S5_HALF), 1) < S5_STATE

    def step(k, carry):
        lo = pl.multiple_of(k * SUBLANES, SUBLANES)
        hi = pl.multiple_of(n_chunk - SUBLANES - k * SUBLANES, SUBLANES)
        out = []
        for b in range(n_seq):
            hre, him = carry[2 * b], carry[2 * b + 1]
            res, ims = [], []
            for j in range(SUBLANES):
                res.append(hre)
                ims.append(him)
                sf = s_ref[pl.ds(b * n_chunk + lo + j, 1), :]
                sb = s_ref[pl.ds(b * n_chunk + hi + SUBLANES - 1 - j, 1), :]
                sre = jnp.where(is_fwd, sf[:, :S5_HALF], sb[:, :S5_HALF])
                sim = jnp.where(is_fwd, sf[:, S5_HALF:], sb[:, S5_HALF:])
                hre, him = ar * hre - ai * him + sre, ar * him + ai * hre + sim
            rows_f = pl.ds(pl.multiple_of(b * n_chunk + lo, SUBLANES), SUBLANES)
            rows_b = pl.ds(pl.multiple_of(b * n_chunk + hi, SUBLANES), SUBLANES)
            hf_ref[rows_f, 0:S5_HALF] = _rows_to_tile(res, reverse=False)
            hf_ref[rows_f, S5_HALF:] = _rows_to_tile(ims, reverse=False)
            hb_ref[rows_b, 0:S5_HALF] = _rows_to_tile(res, reverse=True)
            hb_ref[rows_b, S5_HALF:] = _rows_to_tile(ims, reverse=True)
            out += [hre, him]
        return tuple(out)

    init = tuple(jnp.zeros((1, S5_HALF), F32) for _ in range(2 * n_seq))
    lax.fori_loop(0, n_chunk // SUBLANES, step, init)

    lane = lax.broadcasted_iota(jnp.int32, (1, 2 * S5_HALF), 1)
    fwd_lane = (lane % S5_HALF) < S5_STATE
    h = jnp.where(fwd_lane, hf_ref[...], hb_ref[...]).astype(BF16)
    y = (jnp.dot(u, m_ref[...], preferred_element_type=F32)
         + jnp.dot(h, v_ref[...], preferred_element_type=F32))
    yt_ref[...] = y.T


def _s5_scan(ut, w, m, v, a_t, n_seq):
    groups, vec, rows = ut.shape
    n_chunk = rows // n_seq
    return pl.pallas_call(
        functools.partial(_s5_kernel, n_seq=n_seq, n_chunk=n_chunk),
        grid=(groups,),
        in_specs=[
            pl.BlockSpec((None, vec, rows), lambda g: (g, 0, 0)),
            pl.BlockSpec((None, vec, 2 * S5_HALF), lambda g: (g, 0, 0)),
            pl.BlockSpec((None, vec, vec), lambda g: (g, 0, 0)),
            pl.BlockSpec((None, 2 * S5_HALF, vec), lambda g: (g, 0, 0)),
            pl.BlockSpec((None, 2, S5_HALF), lambda g: (g, 0, 0)),
        ],
        out_specs=pl.BlockSpec((None, vec, rows), lambda g: (g, 0, 0)),
        out_shape=jax.ShapeDtypeStruct((groups, vec, rows), F32),
        scratch_shapes=[
            pltpu.VMEM((rows, 2 * S5_HALF), F32),
            pltpu.VMEM((rows, 2 * S5_HALF), F32),
            pltpu.VMEM((rows, 2 * S5_HALF), F32),
        ],
        compiler_params=_params("arbitrary"),
        name="s5_scan",
    )(ut, w, m, v, a_t)


def _to_groups_kernel(x_ref, o_ref):
    cols, rows = x_ref.shape[1], x_ref.shape[0]
    o_ref[...] = x_ref[...].T.reshape(cols // S5_GROUP, S5_GROUP, rows)


def _from_groups_kernel(y_ref, o_ref):
    groups, _, rows = y_ref.shape
    o_ref[...] = y_ref[...].reshape(groups * S5_GROUP, rows).T


def _s5_to_groups(x_chunks, width, row_blocks_per_tau, tr, tcol):
    n_rows = x_chunks.shape[0]
    n_cb = width // tcol
    return pl.pallas_call(
        _to_groups_kernel,
        grid=(n_rows // tr, S5_CHUNK, n_cb),
        in_specs=[pl.BlockSpec((tr, tcol), lambda i, t, c: (i, t * row_blocks_per_tau + c))],
        out_specs=pl.BlockSpec((tcol // S5_GROUP, S5_GROUP, tr), lambda i, t, c: (c, t, i)),
        out_shape=jax.ShapeDtypeStruct((width // S5_GROUP, S5_VEC, n_rows), F32),
        compiler_params=_params("arbitrary", "arbitrary", "arbitrary"),
        name="s5_to_groups",
    )(x_chunks)


def _s5_from_groups(yt, tr, tcol):
    groups, _, n_rows = yt.shape
    width = groups * S5_GROUP
    n_cb = width // tcol
    return pl.pallas_call(
        _from_groups_kernel,
        grid=(n_rows // tr, S5_CHUNK, n_cb),
        in_specs=[pl.BlockSpec((tcol // S5_GROUP, S5_GROUP, tr), lambda i, t, c: (c, t, i))],
        out_specs=pl.BlockSpec((tr, tcol), lambda i, t, c: (i, t * n_cb + c)),
        out_shape=jax.ShapeDtypeStruct((n_rows, S5_CHUNK * width), F32),
        compiler_params=_params("arbitrary", "arbitrary", "arbitrary"),
        name="s5_from_groups",
    )(yt)


def _s5_operators(lam_re, lam_im, log_dt, b_re, b_im, c_re, c_im, d_skip):
    hi = lax.Precision.HIGHEST
    t = S5_CHUNK
    groups = lam_re.shape[1]
    lr = jnp.minimum(lam_re.astype(F32), -1e-4)
    li = lam_im.astype(F32)
    dt = jnp.exp(log_dt.astype(F32))[..., None]
    mag = jnp.exp(lr * dt)
    ab_re = mag * jnp.cos(li * dt)
    ab_im = mag * jnp.sin(li * dt)
    nr, ni = ab_re - 1.0, ab_im
    den = lr * lr + li * li
    f_re = (nr * lr + ni * li) / den
    f_im = (ni * lr - nr * li) / den
    br, bi = b_re.astype(F32), b_im.astype(F32)
    bb_re = f_re[..., None] * br - f_im[..., None] * bi
    bb_im = f_re[..., None] * bi + f_im[..., None] * br
    cr, ci = c_re.astype(F32), c_im.astype(F32)

    k = jnp.arange(t + 1, dtype=F32)
    mag_k = jnp.exp((lr * dt)[..., None] * k)
    ang_k = (li * dt)[..., None] * k
    pr = mag_k * jnp.cos(ang_k)
    pi = mag_k * jnp.sin(ang_k)

    ab_r = pr[..., :t, None] * bb_re[..., None, :] - pi[..., :t, None] * bb_im[..., None, :]
    ab_i = pr[..., :t, None] * bb_im[..., None, :] + pi[..., :t, None] * bb_re[..., None, :]
    lag = (jnp.einsum('zgnp,zgpkm->zgknm', cr, ab_r, precision=hi)
           - jnp.einsum('zgnp,zgpkm->zgknm', ci, ab_i, precision=hi))

    s_idx = jnp.arange(t)[:, None]
    t_idx = jnp.arange(t)[None, :]
    diff = t_idx - s_idx
    fwd = jnp.where((diff >= 0)[None, :, :, None, None], lag[0][:, jnp.clip(diff, 0, t - 1)], 0.0)
    bwd = jnp.where((diff <= 0)[None, :, :, None, None], lag[1][:, jnp.clip(-diff, 0, t - 1)], 0.0)
    skip = (jnp.eye(t, dtype=F32)[None, :, :, None, None]
            * (d_skip.astype(F32).reshape(groups, S5_GROUP)[:, None, None, :, None]
               * jnp.eye(S5_GROUP, dtype=F32)[None, None, None]))
    m_op = (fwd + bwd + skip).transpose(0, 1, 4, 2, 3).reshape(groups, S5_VEC, S5_VEC)

    pf_r, pf_i = pr[0][..., :t][..., ::-1], pi[0][..., :t][..., ::-1]
    pb_r, pb_i = pr[1][..., :t], pi[1][..., :t]

    def contrib(p_r, p_i, z):
        re = p_r[..., None] * bb_re[z][:, :, None, :] - p_i[..., None] * bb_im[z][:, :, None, :]
        im = p_r[..., None] * bb_im[z][:, :, None, :] + p_i[..., None] * bb_re[z][:, :, None, :]
        to_rows = lambda x: x.transpose(0, 2, 3, 1).reshape(groups, S5_VEC, S5_STATE)
        return to_rows(re), to_rows(im)

    wf_re, wf_im = contrib(pf_r, pf_i, 0)
    wb_re, wb_im = contrib(pb_r, pb_i, 1)
    w_op = jnp.concatenate([wf_re, wb_re, wf_im, wb_im], axis=-1)

    qf_r, qf_i = pr[0][..., 1:], pi[0][..., 1:]
    qb_r, qb_i = pr[1][..., 1:][..., ::-1], pi[1][..., 1:][..., ::-1]

    def readout(q_r, q_i, z):
        c_r = cr[z].transpose(0, 2, 1)[:, :, None, :]
        c_i = ci[z].transpose(0, 2, 1)[:, :, None, :]
        w_r = c_r * q_r[..., None] - c_i * q_i[..., None]
        w_i = c_r * q_i[..., None] + c_i * q_r[..., None]
        return w_r.reshape(groups, S5_STATE, S5_VEC), (-w_i).reshape(groups, S5_STATE, S5_VEC)

    vf_re, vf_im = readout(qf_r, qf_i, 0)
    vb_re, vb_im = readout(qb_r, qb_i, 1)
    v_op = jnp.concatenate([vf_re, vb_re, vf_im, vb_im], axis=1)

    a_t = jnp.stack([jnp.concatenate([pr[0][..., t], pr[1][..., t]], axis=-1),
                     jnp.concatenate([pi[0][..., t], pi[1][..., t]], axis=-1)], axis=1)
    return w_op.astype(BF16), m_op.astype(BF16), v_op.astype(BF16), a_t


def _rope(x, cos, sin_lo, sin_hi):
    return (x * cos
            + pltpu.roll(x, HEAD_DIM - ROPE_HALF, 1) * sin_lo
            + pltpu.roll(x, ROPE_HALF, 1) * sin_hi)


def _attn_kernel(sink_ref, q_ref, k_ref, v_ref, qn_ref, kn_ref, cos_ref, slo_ref, shi_ref,
                 o_ref, kb_ref, vb_ref, *, seq):
    h = pl.program_id(1)
    n = pl.program_id(2)
    prep_rows = 256

    @pl.when(n == 0)
    def _():
        pad = jnp.zeros((BLOCK, HEAD_DIM), BF16)
        kb_ref[0:BLOCK, :] = pad
        vb_ref[0:BLOCK, :] = pad
        kb_ref[BLOCK + seq:, :] = pad
        vb_ref[BLOCK + seq:, :] = pad

        def prep(c, carry):
            r0 = pl.multiple_of(c * prep_rows, prep_rows)
            rows = pl.ds(r0, prep_rows)
            k = _rms_normalize(k_ref[rows, :], kn_ref[...])
            k = _rope(k, cos_ref[rows, :], slo_ref[rows, :], shi_ref[rows, :])
            kb_ref[pl.ds(r0 + BLOCK, prep_rows), :] = k.astype(BF16)
            vb_ref[pl.ds(r0 + BLOCK, prep_rows), :] = v_ref[rows, :].astype(BF16)
            return carry

        lax.fori_loop(0, seq // prep_rows, prep, 0)

    r0 = pl.multiple_of(n * BLOCK, BLOCK)
    rows = pl.ds(r0, BLOCK)
    cos, slo, shi = cos_ref[rows, :], slo_ref[rows, :], shi_ref[rows, :]
    heads = []
    for g in range(Q_PER_KV):
        qg = _rms_normalize(q_ref[:, g * HEAD_DIM:(g + 1) * HEAD_DIM], qn_ref[...])
        heads.append(_rope(qg, cos, slo, shi).astype(BF16))
    q4 = jnp.concatenate(heads, axis=0)
    kw = kb_ref[pl.ds(r0, 3 * BLOCK), :]
    vw = vb_ref[pl.ds(r0, 3 * BLOCK), :]
    s = lax.dot_general(q4, kw, (((1,), (1,)), ((), ())), preferred_element_type=F32)
    s = s * (HEAD_DIM ** -0.5)

    qi = lax.broadcasted_iota(jnp.int32, (BLOCK, 1), 0)
    kj = lax.broadcasted_iota(jnp.int32, (BLOCK, 3 * BLOCK), 1)
    lo = jnp.maximum(qi + (BLOCK - WINDOW), BLOCK - r0)
    hi = jnp.minimum(qi + (BLOCK + WINDOW), seq - 1 + BLOCK - r0)
    probs = []
    inv_denoms = []
    for g in range(Q_PER_KV):
        sg = s[g * BLOCK:(g + 1) * BLOCK, :]
        sg = jnp.where(kj >= lo, jnp.where(kj <= hi, sg, NEG_INF), NEG_INF)
        sk = sink_ref[h * Q_PER_KV + g]
        mx = jnp.maximum(jnp.max(sg, axis=-1, keepdims=True), sk)
        p = jnp.exp(sg - mx)
        denom = jnp.sum(p, axis=-1, keepdims=True) + jnp.exp(sk - mx)
        probs.append(p.astype(BF16))
        inv_denoms.append(1.0 / denom)
    p4 = jnp.concatenate(probs, axis=0)
    o4 = jnp.dot(p4, vw, preferred_element_type=F32)
    for g in range(Q_PER_KV):
        og = o4[g * BLOCK:(g + 1) * BLOCK, :] * inv_denoms[g]
        o_ref[:, g * HEAD_DIM:(g + 1) * HEAD_DIM] = og.astype(o_ref.dtype)


def _attention(proj, sink, q_norm, k_norm, cos, sin_lo, sin_hi, q_col0, k_col0, v_col0):
    bsz, seq, _ = proj.shape
    q_w = Q_PER_KV * HEAD_DIM
    q_blk0, k_blk0, v_blk0 = q_col0 // q_w, k_col0 // HEAD_DIM, v_col0 // HEAD_DIM
    table = pl.BlockSpec((seq, HEAD_DIM), lambda b, h, n: (0, 0))
    return pl.pallas_call(
        functools.partial(_attn_kernel, seq=seq),
        grid=(bsz, N_KV_HEADS, seq // BLOCK),
        in_specs=[
            pl.BlockSpec(memory_space=pltpu.SMEM),
            pl.BlockSpec((None, BLOCK, q_w), lambda b, h, n: (b, n, q_blk0 + h)),
            pl.BlockSpec((None, seq, HEAD_DIM), lambda b, h, n: (b, 0, k_blk0 + h)),
            pl.BlockSpec((None, seq, HEAD_DIM), lambda b, h, n: (b, 0, v_blk0 + h)),
            pl.BlockSpec((1, HEAD_DIM), lambda b, h, n: (0, 0)),
            pl.BlockSpec((1, HEAD_DIM), lambda b, h, n: (0, 0)),
            table, table, table,
        ],
        out_specs=pl.BlockSpec((None, BLOCK, q_w), lambda b, h, n: (b, n, h)),
        out_shape=jax.ShapeDtypeStruct((bsz, seq, N_KV_HEADS * q_w), BF16),
        scratch_shapes=[
            pltpu.VMEM((seq + 2 * BLOCK, HEAD_DIM), BF16),
            pltpu.VMEM((seq + 2 * BLOCK, HEAD_DIM), BF16),
        ],
        compiler_params=_params("arbitrary", "arbitrary", "arbitrary"),
        name="window_attention",
    )(sink, proj, proj, proj, q_norm.reshape(1, HEAD_DIM), k_norm.reshape(1, HEAD_DIM), cos, sin_lo, sin_hi)


def _rope_tables(seq):
    inv = ROPE_THETA ** (-jnp.arange(ROPE_HALF, dtype=F32) / ROPE_HALF)
    ang = jnp.arange(seq).astype(F32)[:, None] * inv[None, :]
    cos, sin = jnp.cos(ang), jnp.sin(ang)
    rest = HEAD_DIM - 2 * ROPE_HALF
    cos_t = jnp.concatenate([cos, cos, jnp.ones((seq, rest), F32)], axis=-1)
    sin_lo = jnp.concatenate([-sin, jnp.zeros((seq, HEAD_DIM - ROPE_HALF), F32)], axis=-1)
    sin_hi = jnp.concatenate([jnp.zeros((seq, ROPE_HALF), F32), sin, jnp.zeros((seq, rest), F32)], axis=-1)
    return cos_t, sin_lo, sin_hi


def _rglru_kernel(gate_ref, xb_ref, cw_ref, cb_ref, wf_ref, wb_ref, bias_ref, lam_ref, o_ref,
                  xi_ref, conv_ref, hf_ref, pf_ref, hb_ref, pb_ref, af_ref, bf_ref, ab_ref, bb_ref, *, seq, tch):
    seg_len = seq // SUBLANES
    n_slab = LRU_BS // LANES
    halo = CONV_LEFT * SUBLANES
    n_chunk = seg_len // tch
    chunk_rows = tch * SUBLANES

    def interleave(tb, carry):
        t0 = tb * SUBLANES
        for s in range(SUBLANES):
            tile = xb_ref[pl.ds(pl.multiple_of(s * seg_len + t0, SUBLANES), SUBLANES), :]
            dst = pl.ds(halo + t0 * SUBLANES + s, SUBLANES, stride=SUBLANES)
            for slab in range(n_slab):
                xi_ref[slab, dst, :] = tile[:, slab * LANES:(slab + 1) * LANES]
        return carry

    lax.fori_loop(0, seg_len // SUBLANES, interleave, 0)

    sub = lax.broadcasted_iota(jnp.int32, (SUBLANES, LANES), 0)
    for slab in range(n_slab):
        for back in range(1, CONV_LEFT + 1):
            src = halo + (seg_len - back) * SUBLANES
            tile = pltpu.roll(xi_ref[slab, src:src + SUBLANES, :], 1, 0)
            xi_ref[slab, halo - back * SUBLANES:halo - (back - 1) * SUBLANES, :] = jnp.where(sub == 0, 0.0, tile)
        tile = pltpu.roll(xi_ref[slab, halo:halo + SUBLANES, :], SUBLANES - 1, 0)
        end = halo + seq
        xi_ref[slab, end:end + SUBLANES, :] = jnp.where(sub == SUBLANES - 1, 0.0, tile)

    cw = cw_ref[...]
    cb = cb_ref[...]

    def conv_chunk(c, carry):
        t0 = c * tch
        parts = []
        for slab in range(n_slab):
            cols = slice(slab * LANES, (slab + 1) * LANES)
            acc = cb[:, cols]
            for tap in range(CONV_WIDTH):
                r0 = pl.multiple_of((t0 + tap) * SUBLANES, SUBLANES)
                acc = acc + cw[tap:tap + 1, cols] * xi_ref[slab, pl.ds(r0, chunk_rows), :]
            parts.append(acc)
        conv_ref[pl.ds(pl.multiple_of(t0 * SUBLANES, SUBLANES), chunk_rows), :] = jnp.concatenate(parts, axis=1)
        return carry

    lax.fori_loop(0, n_chunk, conv_chunk, 0)

    def softplus(x):
        return jnp.maximum(x, 0.0) + jnp.log1p(jnp.exp(-jnp.abs(x)))

    def gates(t0, w_ref, z, a_out, b_out):
        half_conv = conv_ref[pl.ds(pl.multiple_of(t0 * SUBLANES, SUBLANES), chunk_rows), :]
        pre = jnp.dot(half_conv.astype(BF16), w_ref[...], preferred_element_type=F32)
        t_r = jnp.tanh(pre[:, :LRU_BS] + bias_ref[2 * z:2 * z + 1, :])
        t_i = jnp.tanh(pre[:, LRU_BS:] + bias_ref[2 * z + 1:2 * z + 2, :])
        c1 = (-0.5 * LRU_C) * softplus(-lam_ref[z:z + 1, :])
        a = jnp.exp(c1 * t_r + c1)
        a_out[...] = a
        v = 1.0 - a * a
        root = jnp.where(v > 0.0, v * lax.rsqrt(v), 0.0)
        b_out[...] = root * (half_conv * t_i + half_conv)

    def chunk_pair(c, carry):
        tf0 = c * tch
        tb0 = (n_chunk - 1 - c) * tch
        gates(tf0, wf_ref, 0, af_ref, bf_ref)
        gates(tb0, wb_ref, 1, ab_ref, bb_ref)

        def steps(k, state):
            hf, pf, hb, pb = state
            group = SUBLANES * SUBLANES
            src0 = pl.multiple_of(k * group, group)
            dst0 = pl.multiple_of(tf0 * SUBLANES + k * group, SUBLANES)
            src0_b = pl.multiple_of((tch - 1) * SUBLANES - k * group, SUBLANES)
            dst0_b = pl.multiple_of((tb0 + tch - 1) * SUBLANES - k * group, SUBLANES)
            for j in range(SUBLANES):
                src = pl.ds(src0 + j * SUBLANES, SUBLANES)
                a = af_ref[src, :]
                hf = a * hf + bf_ref[src, :]
                pf = a * pf
                dst = pl.ds(dst0 + j * SUBLANES, SUBLANES)
                src_b = pl.ds(src0_b - j * SUBLANES, SUBLANES)
                a = ab_ref[src_b, :]
                hb = a * hb + bb_ref[src_b, :]
                pb = a * pb
                dst_b = pl.ds(dst0_b - j * SUBLANES, SUBLANES)
                for slab in range(n_slab):
                    cols = slice(slab * LANES, (slab + 1) * LANES)
                    hf_ref[slab, dst, :] = hf[:, cols]
                    pf_ref[slab, dst, :] = pf[:, cols]
                    hb_ref[slab, dst_b, :] = hb[:, cols]
                    pb_ref[slab, dst_b, :] = pb[:, cols]
            return hf, pf, hb, pb

        return lax.fori_loop(0, tch // SUBLANES, steps, carry)

    zeros = jnp.zeros((SUBLANES, LRU_BS), F32)
    ones = jnp.ones((SUBLANES, LRU_BS), F32)
    end_f, prod_f, end_b, prod_b = lax.fori_loop(0, n_chunk, chunk_pair, (zeros, ones, zeros, ones))

    cin_f = [jnp.zeros((1, LRU_BS), F32)]
    for s in range(SUBLANES - 1):
        cin_f.append(end_f[s:s + 1, :] + prod_f[s:s + 1, :] * cin_f[s])
    cin_b = [jnp.zeros((1, LRU_BS), F32)]
    for s in range(SUBLANES - 1, 0, -1):
        cin_b.insert(0, end_b[s:s + 1, :] + prod_b[s:s + 1, :] * cin_b[0])

    out_rows = 2 * SUBLANES

    def finish(tb, carry):
        for s in range(SUBLANES):
            halves = []
            for half in range(2):
                t0 = tb * out_rows + half * SUBLANES
                src = pl.ds(t0 * SUBLANES + s, SUBLANES, stride=SUBLANES)
                parts = []
                for slab in range(n_slab):
                    cols = slice(slab * LANES, (slab + 1) * LANES)
                    parts.append(hf_ref[slab, src, :] + pf_ref[slab, src, :] * cin_f[s][:, cols]
                                 + hb_ref[slab, src, :] + pb_ref[slab, src, :] * cin_b[s][:, cols])
                halves.append(jnp.concatenate(parts, axis=1))
            rows = pl.ds(pl.multiple_of(s * seg_len + tb * out_rows, out_rows), out_rows)
            y = jnp.concatenate(halves, axis=0) * jax.nn.gelu(gate_ref[rows, :])
            o_ref[rows, :] = y.astype(o_ref.dtype)
        return carry

    lax.fori_loop(0, seg_len // out_rows, finish, 0)


def _rglru(proj, conv_w, conv_b, w_fwd, w_bwd, bias, lam, tch):
    bsz, seq, width2 = proj.shape
    n_blk = width2 // (2 * LRU_BS)
    n_slab = LRU_BS // LANES
    seq_blk = lambda off: pl.BlockSpec((None, seq, LRU_BS), lambda b, n: (b, 0, off + n))
    per_blk = lambda rows, cols: pl.BlockSpec((None, rows, cols), lambda b, n: (n, 0, 0))
    slab_rows = pltpu.VMEM((n_slab, seq, LANES), F32)
    gate_rows = pltpu.VMEM((tch * SUBLANES, LRU_BS), F32)
    return pl.pallas_call(
        functools.partial(_rglru_kernel, seq=seq, tch=tch),
        grid=(bsz, n_blk),
        in_specs=[
            seq_blk(0),
            seq_blk(n_blk),
            pl.BlockSpec((CONV_WIDTH, LRU_BS), lambda b, n: (0, n)),
            pl.BlockSpec((1, LRU_BS), lambda b, n: (0, n)),
            per_blk(LRU_BS, 2 * LRU_BS),
            per_blk(LRU_BS, 2 * LRU_BS),
            per_blk(4, LRU_BS),
            per_blk(2, LRU_BS),
        ],
        out_specs=pl.BlockSpec((None, seq, LRU_BS), lambda b, n: (b, 0, n)),
        out_shape=jax.ShapeDtypeStruct((bsz, seq, n_blk * LRU_BS), BF16),
        scratch_shapes=[
            pltpu.VMEM((n_slab, seq + (CONV_WIDTH - 1) * SUBLANES, LANES), F32),
            pltpu.VMEM((seq, LRU_BS), F32),
            slab_rows, slab_rows, slab_rows, slab_rows,
            gate_rows, gate_rows, gate_rows, gate_rows,
        ],
        compiler_params=_params("arbitrary", "arbitrary"),
        name="rglru",
    )(proj, proj, conv_w, conv_b.reshape(1, -1), w_fwd, w_bwd, bias, lam)


def _even_layer(x, p, bsz, seq):
    m, d = x.shape
    s5_width = p["s5_w_glu"].shape[0]
    attn_width = N_KV_HEADS * Q_PER_KV * HEAD_DIM
    kv_width = N_KV_HEADS * HEAD_DIM
    proj = _norm_matmul(x, p["norm_mix"], p["w_in"], tm=1024, tn=512)

    n_chunk_rows = m // S5_CHUNK
    relayout_cols = 512
    ut = _s5_to_groups(proj.reshape(n_chunk_rows, S5_CHUNK * proj.shape[1]), s5_width,
                       proj.shape[1] // relayout_cols, tr=512, tcol=relayout_cols)
    yt = _s5_scan(ut, p["s5_w"], p["s5_m"], p["s5_v"], p["s5_a"], n_seq=bsz)
    y = _s5_from_groups(yt, tr=512, tcol=relayout_cols).reshape(m, s5_width)
    y_s5 = _glu(y, p["s5_w_glu"], p["s5_b_glu"], tm=512)

    y_attn = _attention(proj.reshape(bsz, seq, -1), p["attn_sink"], p["attn_q_norm"], p["attn_k_norm"],
                        p["rope_cos"], p["rope_sin_lo"], p["rope_sin_hi"],
                        q_col0=s5_width, k_col0=s5_width + attn_width,
                        v_col0=s5_width + attn_width + kv_width)
    return _proj_residual(x, [y_s5, y_attn.reshape(m, attn_width)],
                          [p["w_out"][:s5_width], p["w_out"][s5_width:]], tm=1024, tn=512)


def _odd_layer(x, p, bsz, seq):
    m, d = x.shape
    proj = _norm_matmul(x, p["norm_mix"], p["w_in"], tm=1024, tn=512)
    y = _rglru(proj.reshape(bsz, seq, -1), p["conv_w"], p["conv_b"], p["w_fwd"], p["w_bwd"],
               p["gate_bias"], p["lam"], tch=64)
    return _proj_residual(x, [y.reshape(m, -1)], [p["w_out"]], tm=1024, tn=512)


def kernel(x_prompt, x_sample, norm_mix, norm_ffn, ev_w_in, ev_w_out, s5_lam_re, s5_lam_im, s5_log_dt, s5_b_re, s5_b_im, s5_c_re, s5_c_im, s5_d, s5_w_glu, s5_b_glu, attn_q_norm, attn_k_norm, attn_sink, od_w_in, od_w_out, lru_conv_w, lru_conv_b, lru_wa, lru_ba, lru_wx, lru_bx, lru_lam, ffn_w1, ffn_w3, ffn_w2):
    depth = norm_mix.shape[0]
    seq_lens = {x_prompt.shape[1], x_sample.shape[1]}
    rope = {s: _rope_tables(s) for s in seq_lens}

    layers = []
    for layer in range(depth):
        if layer % 2 == 0:
            e = layer // 2
            s5_w, s5_m, s5_v, s5_a = _s5_operators(s5_lam_re[e], s5_lam_im[e], s5_log_dt[e], s5_b_re[e],
                                                   s5_b_im[e], s5_c_re[e], s5_c_im[e], s5_d[e])
            mix = dict(norm_mix=norm_mix[layer], w_in=ev_w_in[e].astype(BF16), w_out=ev_w_out[e].astype(BF16),
                       s5_w=s5_w, s5_m=s5_m, s5_v=s5_v, s5_a=s5_a,
                       s5_w_glu=s5_w_glu[e].astype(BF16), s5_b_glu=s5_b_glu[e],
                       attn_q_norm=attn_q_norm[e], attn_k_norm=attn_k_norm[e], attn_sink=attn_sink[e])
        else:
            o = layer // 2
            n_blk = lru_wa.shape[2]
            mix = dict(norm_mix=norm_mix[layer], w_in=od_w_in[o].astype(BF16), w_out=od_w_out[o].astype(BF16),
                       conv_w=0.5 * lru_conv_w[o], conv_b=0.5 * lru_conv_b[o],
                       w_fwd=jnp.concatenate([lru_wa[o, 0], lru_wx[o, 0]], axis=-1).astype(BF16),
                       w_bwd=jnp.concatenate([lru_wa[o, 1], lru_wx[o, 1]], axis=-1).astype(BF16),
                       gate_bias=(0.5 * jnp.stack([lru_ba[o, 0], lru_bx[o, 0], lru_ba[o, 1], lru_bx[o, 1]], axis=0))
                       .reshape(4, n_blk, LRU_BS).transpose(1, 0, 2),
                       lam=lru_lam[o].reshape(2, n_blk, LRU_BS).transpose(1, 0, 2))
        ffn = dict(gain=norm_ffn[layer], w1=ffn_w1[layer].astype(BF16), w3=ffn_w3[layer].astype(BF16),
                   w2=ffn_w2[layer].astype(BF16))
        layers.append((mix, ffn))

    def run(x):
        bsz, seq, d = x.shape
        h = x.reshape(bsz * seq, d)
        for layer, (mix, ffn) in enumerate(layers):
            if layer % 2 == 0:
                cos, sin_lo, sin_hi = rope[seq]
                h = _even_layer(h, dict(mix, rope_cos=cos, rope_sin_lo=sin_lo, rope_sin_hi=sin_hi), bsz, seq)
            else:
                h = _odd_layer(h, mix, bsz, seq)
            h = _ffn(h, ffn["gain"], ffn["w1"], ffn["w3"], ffn["w2"], tm=512, tf=512)
        return h.reshape(bsz, seq, d)

    return (run(x_prompt), run(x_sample))
```

```python
import functools
import math

import jax
import jax.numpy as jnp
from jax import lax
from jax.experimental import pallas as pl
from jax.experimental.pallas import tpu as pltpu

F32 = jnp.float32
BF16 = jnp.bfloat16

LANES = 128
SUBLANES = 8
VMEM_LIMIT_BYTES = 56 * 1024 * 1024

EPS = 1e-6
NEG_INF = -1e30
S5_GROUP = 16
S5_STATE = 64
HEAD_DIM = 128
N_KV_HEADS = 2
Q_PER_KV = 4
WINDOW = 128
BLOCK = 128
ROPE_HALF = 16
ROPE_THETA = 500000.0
LRU_BS = 256
LRU_C = 8.0
CONV_WIDTH = 4
CONV_LEFT = 2

S5_CHUNK = 16
S5_VEC = S5_CHUNK * S5_GROUP
S5_HALF = 2 * S5_STATE


def _params(*semantics):
    return pltpu.CompilerParams(dimension_semantics=semantics, vmem_limit_bytes=VMEM_LIMIT_BYTES)


def _rows_to_tile(rows, reverse):
    n = rows[0].shape[1]
    sub = lax.broadcasted_iota(jnp.int32, (SUBLANES, n), 0)
    tile = jnp.broadcast_to(rows[0], (SUBLANES, n))
    for j in range(1, SUBLANES):
        tile = jnp.where(sub == (SUBLANES - 1 - j if reverse else j), jnp.broadcast_to(rows[j], (SUBLANES, n)), tile)
    return tile


def _rms_normalize(x, gain):
    ms = jnp.mean(x * x, axis=-1, keepdims=True)
    return x * lax.rsqrt(ms + EPS) * gain


def _norm_matmul_kernel(x_ref, g_ref, w_ref, o_ref, hn_ref):
    @pl.when(pl.program_id(1) == 0)
    def _():
        hn_ref[...] = _rms_normalize(x_ref[...], g_ref[...]).astype(BF16)

    o_ref[...] = jnp.dot(hn_ref[...], w_ref[...], preferred_element_type=F32)


def _norm_matmul(x, gain, w, tm, tn):
    m, d = x.shape
    n = w.shape[1]
    return pl.pallas_call(
        _norm_matmul_kernel,
        grid=(m // tm, n // tn),
        in_specs=[
            pl.BlockSpec((tm, d), lambda i, j: (i, 0)),
            pl.BlockSpec((1, d), lambda i, j: (0, 0)),
            pl.BlockSpec((d, tn), lambda i, j: (0, j)),
        ],
        out_specs=pl.BlockSpec((tm, tn), lambda i, j: (i, j)),
        out_shape=jax.ShapeDtypeStruct((m, n), F32),
        scratch_shapes=[pltpu.VMEM((tm, d), BF16)],
        compiler_params=_params("arbitrary", "arbitrary"),
        name="norm_matmul",
    )(x, gain.reshape(1, d), w)


def _proj_residual_kernel(*refs, n_in):
    x_ref = refs[0]
    a_refs = refs[1:1 + n_in]
    w_refs = refs[1 + n_in:1 + 2 * n_in]
    o_ref = refs[1 + 2 * n_in]
    acc = x_ref[...]
    for a_ref, w_ref in zip(a_refs, w_refs):
        acc = acc + jnp.dot(a_ref[...], w_ref[...], preferred_element_type=F32)
    o_ref[...] = acc


def _proj_residual(x, acts, weights, tm, tn):
    m, n = x.shape
    n_in = len(acts)
    in_specs = [pl.BlockSpec((tm, tn), lambda i, j: (i, j))]
    in_specs += [pl.BlockSpec((tm, a.shape[1]), lambda i, j: (i, 0)) for a in acts]
    in_specs += [pl.BlockSpec((w.shape[0], tn), lambda i, j: (0, j)) for w in weights]
    return pl.pallas_call(
        functools.partial(_proj_residual_kernel, n_in=n_in),
        grid=(m // tm, n // tn),
        in_specs=in_specs,
        out_specs=pl.BlockSpec((tm, tn), lambda i, j: (i, j)),
        out_shape=jax.ShapeDtypeStruct((m, n), F32),
        compiler_params=_params("arbitrary", "arbitrary"),
        name="proj_residual",
    )(x, *acts, *weights)


def _ffn_kernel(x_ref, g_ref, w1_ref, w3_ref, w2_ref, o_ref, hn_ref):
    j = pl.program_id(1)

    @pl.when(j == 0)
    def _():
        x = x_ref[...]
        hn_ref[...] = _rms_normalize(x, g_ref[...]).astype(BF16)
        o_ref[...] = x

    hn = hn_ref[...]
    a = jnp.dot(hn, w1_ref[...], preferred_element_type=F32)
    b = jnp.dot(hn, w3_ref[...], preferred_element_type=F32)
    act = (a * jax.nn.sigmoid(a) * b).astype(BF16)
    o_ref[...] += jnp.dot(act, w2_ref[...], preferred_element_type=F32)


def _ffn(x, gain, w1, w3, w2, tm, tf):
    m, d = x.shape
    f = w1.shape[1]
    return pl.pallas_call(
        _ffn_kernel,
        grid=(m // tm, f // tf),
        in_specs=[
            pl.BlockSpec((tm, d), lambda i, j: (i, 0)),
            pl.BlockSpec((1, d), lambda i, j: (0, 0)),
            pl.BlockSpec((d, tf), lambda i, j: (0, j)),
            pl.BlockSpec((d, tf), lambda i, j: (0, j)),
            pl.BlockSpec((tf, d), lambda i, j: (j, 0)),
        ],
        out_specs=pl.BlockSpec((tm, d), lambda i, j: (i, 0)),
        out_shape=jax.ShapeDtypeStruct((m, d), F32),
        scratch_shapes=[pltpu.VMEM((tm, d), BF16)],
        compiler_params=_params("arbitrary", "arbitrary"),
        name="ffn",
    )(x, gain.reshape(1, d), w1, w3, w2)


def _glu_kernel(y_ref, w_ref, b_ref, o_ref):
    g = jax.nn.gelu(y_ref[...])
    z = jnp.dot(g.astype(BF16), w_ref[...], preferred_element_type=F32) + b_ref[...]
    o_ref[...] = (g * jax.nn.sigmoid(z)).astype(o_ref.dtype)


def _glu(y, w, b, tm):
    m, c = y.shape
    return pl.pallas_call(
        _glu_kernel,
        grid=(m // tm,),
        in_specs=[
            pl.BlockSpec((tm, c), lambda i: (i, 0)),
            pl.BlockSpec((c, c), lambda i: (0, 0)),
            pl.BlockSpec((1, c), lambda i: (0, 0)),
        ],
        out_specs=pl.BlockSpec((tm, c), lambda i: (i, 0)),
        out_shape=jax.ShapeDtypeStruct((m, c), BF16),
        compiler_params=_params("arbitrary"),
        name="s5_glu",
    )(y, w, b.reshape(1, c))


def _s5_kernel(ut_ref, w_ref, m_ref, v_ref, a_ref, yt_ref, s_ref, hf_ref, hb_ref, *, n_seq, n_chunk):
    u = ut_ref[...].T.astype(BF16)
    s = jnp.dot(u, w_ref[...], preferred_element_type=F32)
    s_ref[0] = s[:, :S5_HALF]
    s_ref[1] = s[:, S5_HALF:]
    ar = a_ref[0:1, :]
    ai = a_ref[1:2, :]
    is_fwd = lax.broadcasted_iota(jnp.int32, (1, S5_HALF), 1) < S5_STATE

    def step(i, carry):
        hre, him = carry
        rows_f = pl.ds(i, n_seq, stride=n_chunk)
        rows_b = pl.ds(n_chunk - 1 - i, n_seq, stride=n_chunk)
        hf_ref[0, rows_f, :] = hre
        hf_ref[1, rows_f, :] = him
        hb_ref[0, rows_b, :] = hre
        hb_ref[1, rows_b, :] = him
        sre = jnp.where(is_fwd, s_ref[0, rows_f, :], s_ref[0, rows_b, :])
        sim = jnp.where(is_fwd, s_ref[1, rows_f, :], s_ref[1, rows_b, :])
        return ar * hre - ai * him + sre, ar * him + ai * hre + sim

    zero = jnp.zeros((n_seq, S5_HALF), F32)
    lax.fori_loop(0, n_chunk, step, (zero, zero), unroll=8)

    h_re = jnp.where(is_fwd, hf_ref[0], hb_ref[0])
    h_im = jnp.where(is_fwd, hf_ref[1], hb_ref[1])
    h = jnp.concatenate([h_re, h_im], axis=1).astype(BF16)
    y = (jnp.dot(u, m_ref[...], preferred_element_type=F32)
         + jnp.dot(h, v_ref[...], preferred_element_type=F32))
    yt_ref[...] = y.T


def _s5_scan(ut, w, m, v, a_t, n_seq):
    groups, vec, rows = ut.shape
    n_chunk = rows // n_seq
    return pl.pallas_call(
        functools.partial(_s5_kernel, n_seq=n_seq, n_chunk=n_chunk),
        grid=(groups,),
        in_specs=[
            pl.BlockSpec((None, vec, rows), lambda g: (g, 0, 0)),
            pl.BlockSpec((None, vec, 2 * S5_HALF), lambda g: (g, 0, 0)),
            pl.BlockSpec((None, vec, vec), lambda g: (g, 0, 0)),
            pl.BlockSpec((None, 2 * S5_HALF, vec), lambda g: (g, 0, 0)),
            pl.BlockSpec((None, 2, S5_HALF), lambda g: (g, 0, 0)),
        ],
        out_specs=pl.BlockSpec((None, vec, rows), lambda g: (g, 0, 0)),
        out_shape=jax.ShapeDtypeStruct((groups, vec, rows), F32),
        scratch_shapes=[
            pltpu.VMEM((2, rows, S5_HALF), F32),
            pltpu.VMEM((2, rows, S5_HALF), F32),
            pltpu.VMEM((2, rows, S5_HALF), F32),
        ],
        compiler_params=_params("arbitrary"),
        name="s5_scan",
    )(ut, w, m, v, a_t)


GROUPS_PER_LANE_BLOCK = LANES // S5_GROUP


def _to_groups_kernel(x_ref, o_ref):
    n_chunk = o_ref.shape[2]
    for tau in range(S5_CHUNK):
        rows = x_ref[pl.ds(tau, n_chunk, stride=S5_CHUNK), :]
        o_ref[:, tau * S5_GROUP:(tau + 1) * S5_GROUP, :] = rows.T.reshape(GROUPS_PER_LANE_BLOCK, S5_GROUP, n_chunk)


def _from_groups_kernel(y_ref, o_ref):
    n_chunk = y_ref.shape[2]
    for tau in range(S5_CHUNK):
        slab = y_ref[:, tau * S5_GROUP:(tau + 1) * S5_GROUP, :].reshape(LANES, n_chunk)
        o_ref[pl.ds(tau, n_chunk, stride=S5_CHUNK), :] = slab.T


def _s5_to_groups(x, width, chunks_per_step):
    tokens = x.shape[0]
    n_chunk = tokens // S5_CHUNK
    return pl.pallas_call(
        _to_groups_kernel,
        grid=(n_chunk // chunks_per_step, width // LANES),
        in_specs=[pl.BlockSpec((chunks_per_step * S5_CHUNK, LANES), lambda i, c: (i, c))],
        out_specs=pl.BlockSpec((GROUPS_PER_LANE_BLOCK, S5_VEC, chunks_per_step), lambda i, c: (c, 0, i)),
        out_shape=jax.ShapeDtypeStruct((width // S5_GROUP, S5_VEC, n_chunk), F32),
        compiler_params=_params("arbitrary", "arbitrary"),
        name="s5_to_groups",
    )(x)


def _s5_from_groups(yt, chunks_per_step):
    groups, _, n_chunk = yt.shape
    width = groups * S5_GROUP
    return pl.pallas_call(
        _from_groups_kernel,
        grid=(n_chunk // chunks_per_step, width // LANES),
        in_specs=[pl.BlockSpec((GROUPS_PER_LANE_BLOCK, S5_VEC, chunks_per_step), lambda i, c: (c, 0, i))],
        out_specs=pl.BlockSpec((chunks_per_step * S5_CHUNK, LANES), lambda i, c: (i, c)),
        out_shape=jax.ShapeDtypeStruct((n_chunk * S5_CHUNK, width), F32),
        compiler_params=_params("arbitrary", "arbitrary"),
        name="s5_from_groups",
    )(yt)


def _s5_operators(lam_re, lam_im, log_dt, b_re, b_im, c_re, c_im, d_skip):
    hi = lax.Precision.HIGHEST
    t = S5_CHUNK
    groups = lam_re.shape[1]
    lr = jnp.minimum(lam_re.astype(F32), -1e-4)
    li = lam_im.astype(F32)
    dt = jnp.exp(log_dt.astype(F32))[..., None]
    mag = jnp.exp(lr * dt)
    ab_re = mag * jnp.cos(li * dt)
    ab_im = mag * jnp.sin(li * dt)
    nr, ni = ab_re - 1.0, ab_im
    den = lr * lr + li * li
    f_re = (nr * lr + ni * li) / den
    f_im = (ni * lr - nr * li) / den
    br, bi = b_re.astype(F32), b_im.astype(F32)
    bb_re = f_re[..., None] * br - f_im[..., None] * bi
    bb_im = f_re[..., None] * bi + f_im[..., None] * br
    cr, ci = c_re.astype(F32), c_im.astype(F32)

    k = jnp.arange(t + 1, dtype=F32)
    mag_k = jnp.exp((lr * dt)[..., None] * k)
    ang_k = (li * dt)[..., None] * k
    pr = mag_k * jnp.cos(ang_k)
    pi = mag_k * jnp.sin(ang_k)

    ab_r = pr[..., :t, None] * bb_re[..., None, :] - pi[..., :t, None] * bb_im[..., None, :]
    ab_i = pr[..., :t, None] * bb_im[..., None, :] + pi[..., :t, None] * bb_re[..., None, :]
    lag = (jnp.einsum('zgnp,zgpkm->zgknm', cr, ab_r, precision=hi)
           - jnp.einsum('zgnp,zgpkm->zgknm', ci, ab_i, precision=hi))

    s_idx = jnp.arange(t)[:, None]
    t_idx = jnp.arange(t)[None, :]
    diff = t_idx - s_idx
    fwd = jnp.where((diff >= 0)[None, :, :, None, None], lag[0][:, jnp.clip(diff, 0, t - 1)], 0.0)
    bwd = jnp.where((diff <= 0)[None, :, :, None, None], lag[1][:, jnp.clip(-diff, 0, t - 1)], 0.0)
    skip = (jnp.eye(t, dtype=F32)[None, :, :, None, None]
            * (d_skip.astype(F32).reshape(groups, S5_GROUP)[:, None, None, :, None]
               * jnp.eye(S5_GROUP, dtype=F32)[None, None, None]))
    m_op = (fwd + bwd + skip).transpose(0, 1, 4, 2, 3).reshape(groups, S5_VEC, S5_VEC)

    pf_r, pf_i = pr[0][..., :t][..., ::-1], pi[0][..., :t][..., ::-1]
    pb_r, pb_i = pr[1][..., :t], pi[1][..., :t]

    def contrib(p_r, p_i, z):
        re = p_r[..., None] * bb_re[z][:, :, None, :] - p_i[..., None] * bb_im[z][:, :, None, :]
        im = p_r[..., None] * bb_im[z][:, :, None, :] + p_i[..., None] * bb_re[z][:, :, None, :]
        to_rows = lambda x: x.transpose(0, 2, 3, 1).reshape(groups, S5_VEC, S5_STATE)
        return to_rows(re), to_rows(im)

    wf_re, wf_im = contrib(pf_r, pf_i, 0)
    wb_re, wb_im = contrib(pb_r, pb_i, 1)
    w_op = jnp.concatenate([wf_re, wb_re, wf_im, wb_im], axis=-1)

    qf_r, qf_i = pr[0][..., 1:], pi[0][..., 1:]
    qb_r, qb_i = pr[1][..., 1:][..., ::-1], pi[1][..., 1:][..., ::-1]

    def readout(q_r, q_i, z):
        c_r = cr[z].transpose(0, 2, 1)[:, :, None, :]
        c_i = ci[z].transpose(0, 2, 1)[:, :, None, :]
        w_r = c_r * q_r[..., None] - c_i * q_i[..., None]
        w_i = c_r * q_i[..., None] + c_i * q_r[..., None]
        return w_r.reshape(groups, S5_STATE, S5_VEC), (-w_i).reshape(groups, S5_STATE, S5_VEC)

    vf_re, vf_im = readout(qf_r, qf_i, 0)
    vb_re, vb_im = readout(qb_r, qb_i, 1)
    v_op = jnp.concatenate([vf_re, vb_re, vf_im, vb_im], axis=1)

    a_t = jnp.stack([jnp.concatenate([pr[0][..., t], pr[1][..., t]], axis=-1),
                     jnp.concatenate([pi[0][..., t], pi[1][..., t]], axis=-1)], axis=1)
    return w_op.astype(BF16), m_op.astype(BF16), v_op.astype(BF16), a_t


def _rope(x, cos, sin_lo, sin_hi):
    return (x * cos
            + pltpu.roll(x, HEAD_DIM - ROPE_HALF, 1) * sin_lo
            + pltpu.roll(x, ROPE_HALF, 1) * sin_hi)


def _attn_kernel(sink_ref, q_ref, k_ref, v_ref, qn_ref, kn_ref, cos_ref, slo_ref, shi_ref,
                 o_ref, kb_ref, vb_ref, *, seq, blocks_per_step):
    h = pl.program_id(1)
    n = pl.program_id(2)
    prep_rows = 256

    @pl.when(n == 0)
    def _():
        pad = jnp.zeros((BLOCK, HEAD_DIM), BF16)
        kb_ref[0:BLOCK, :] = pad
        vb_ref[0:BLOCK, :] = pad
        kb_ref[BLOCK + seq:, :] = pad
        vb_ref[BLOCK + seq:, :] = pad

        def prep(c, carry):
            r0 = pl.multiple_of(c * prep_rows, prep_rows)
            rows = pl.ds(r0, prep_rows)
            k = _rms_normalize(k_ref[rows, :], kn_ref[...])
            k = _rope(k, cos_ref[rows, :], slo_ref[rows, :], shi_ref[rows, :])
            kb_ref[pl.ds(r0 + BLOCK, prep_rows), :] = k.astype(BF16)
            vb_ref[pl.ds(r0 + BLOCK, prep_rows), :] = v_ref[rows, :].astype(BF16)
            return carry

        lax.fori_loop(0, seq // prep_rows, prep, 0)

    qi = lax.broadcasted_iota(jnp.int32, (BLOCK, 1), 0)
    kj = lax.broadcasted_iota(jnp.int32, (BLOCK, 3 * BLOCK), 1)

    def query_block(qb, carry):
        local = pl.multiple_of(qb * BLOCK, BLOCK)
        r0 = pl.multiple_of((n * blocks_per_step + qb) * BLOCK, BLOCK)
        rows = pl.ds(r0, BLOCK)
        cos, slo, shi = cos_ref[rows, :], slo_ref[rows, :], shi_ref[rows, :]
        heads = []
        for g in range(Q_PER_KV):
            qg = _rms_normalize(q_ref[pl.ds(local, BLOCK), g * HEAD_DIM:(g + 1) * HEAD_DIM], qn_ref[...])
            heads.append(_rope(qg, cos, slo, shi).astype(BF16))
        q4 = jnp.concatenate(heads, axis=0)
        kw = kb_ref[pl.ds(r0, 3 * BLOCK), :]
        vw = vb_ref[pl.ds(r0, 3 * BLOCK), :]
        s = lax.dot_general(q4, kw, (((1,), (1,)), ((), ())), preferred_element_type=F32)
        s = s * (HEAD_DIM ** -0.5)

        lo = jnp.maximum(qi + (BLOCK - WINDOW), BLOCK - r0)
        hi = jnp.minimum(qi + (BLOCK + WINDOW), seq - 1 + BLOCK - r0)
        probs = []
        inv_denoms = []
        for g in range(Q_PER_KV):
            sg = s[g * BLOCK:(g + 1) * BLOCK, :]
            sg = jnp.where(kj >= lo, jnp.where(kj <= hi, sg, NEG_INF), NEG_INF)
            sk = sink_ref[h * Q_PER_KV + g]
            mx = jnp.maximum(jnp.max(sg, axis=-1, keepdims=True), sk)
            p = jnp.exp(sg - mx)
            denom = jnp.sum(p, axis=-1, keepdims=True) + jnp.exp(sk - mx)
            probs.append(p.astype(BF16))
            inv_denoms.append(1.0 / denom)
        p4 = jnp.concatenate(probs, axis=0)
        o4 = jnp.dot(p4, vw, preferred_element_type=F32)
        for g in range(Q_PER_KV):
            og = o4[g * BLOCK:(g + 1) * BLOCK, :] * inv_denoms[g]
            o_ref[pl.ds(local, BLOCK), g * HEAD_DIM:(g + 1) * HEAD_DIM] = og.astype(o_ref.dtype)
        return carry

    lax.fori_loop(0, blocks_per_step, query_block, 0)


def _attention(proj, sink, q_norm, k_norm, cos, sin_lo, sin_hi, q_col0, k_col0, v_col0, blocks_per_step):
    bsz, seq, _ = proj.shape
    q_w = Q_PER_KV * HEAD_DIM
    q_blk0, k_blk0, v_blk0 = q_col0 // q_w, k_col0 // HEAD_DIM, v_col0 // HEAD_DIM
    table = pl.BlockSpec((seq, HEAD_DIM), lambda b, h, n: (0, 0))
    step_rows = blocks_per_step * BLOCK
    return pl.pallas_call(
        functools.partial(_attn_kernel, seq=seq, blocks_per_step=blocks_per_step),
        grid=(bsz, N_KV_HEADS, seq // step_rows),
        in_specs=[
            pl.BlockSpec(memory_space=pltpu.SMEM),
            pl.BlockSpec((None, step_rows, q_w), lambda b, h, n: (b, n, q_blk0 + h)),
            pl.BlockSpec((None, seq, HEAD_DIM), lambda b, h, n: (b, 0, k_blk0 + h)),
            pl.BlockSpec((None, seq, HEAD_DIM), lambda b, h, n: (b, 0, v_blk0 + h)),
            pl.BlockSpec((1, HEAD_DIM), lambda b, h, n: (0, 0)),
            pl.BlockSpec((1, HEAD_DIM), lambda b, h, n: (0, 0)),
            table, table, table,
        ],
        out_specs=pl.BlockSpec((None, step_rows, q_w), lambda b, h, n: (b, n, h)),
        out_shape=jax.ShapeDtypeStruct((bsz, seq, N_KV_HEADS * q_w), BF16),
        scratch_shapes=[
            pltpu.VMEM((seq + 2 * BLOCK, HEAD_DIM), BF16),
            pltpu.VMEM((seq + 2 * BLOCK, HEAD_DIM), BF16),
        ],
        compiler_params=_params("arbitrary", "arbitrary", "arbitrary"),
        name="window_attention",
    )(sink, proj, proj, proj, q_norm.reshape(1, HEAD_DIM), k_norm.reshape(1, HEAD_DIM), cos, sin_lo, sin_hi)


def _rope_tables(seq):
    inv = ROPE_THETA ** (-jnp.arange(ROPE_HALF, dtype=F32) / ROPE_HALF)
    ang = jnp.arange(seq).astype(F32)[:, None] * inv[None, :]
    cos, sin = jnp.cos(ang), jnp.sin(ang)
    rest = HEAD_DIM - 2 * ROPE_HALF
    cos_t = jnp.concatenate([cos, cos, jnp.ones((seq, rest), F32)], axis=-1)
    sin_lo = jnp.concatenate([-sin, jnp.zeros((seq, HEAD_DIM - ROPE_HALF), F32)], axis=-1)
    sin_hi = jnp.concatenate([jnp.zeros((seq, ROPE_HALF), F32), sin, jnp.zeros((seq, rest), F32)], axis=-1)
    return cos_t, sin_lo, sin_hi


def _rglru_kernel(gate_ref, xb_ref, cw_ref, cb_ref, wf_ref, wb_ref, bias_ref, lam_ref, o_ref,
                  xi_ref, conv_ref, hf_ref, pf_ref, hb_ref, pb_ref, af_ref, bf_ref, ab_ref, bb_ref, *, seq, tch):
    seg_len = seq // SUBLANES
    n_slab = LRU_BS // LANES
    halo = CONV_LEFT * SUBLANES
    n_chunk = seg_len // tch
    chunk_rows = tch * SUBLANES

    def interleave(tb, carry):
        t0 = tb * SUBLANES
        for s in range(SUBLANES):
            tile = xb_ref[pl.ds(pl.multiple_of(s * seg_len + t0, SUBLANES), SUBLANES), :]
            dst = pl.ds(halo + t0 * SUBLANES + s, SUBLANES, stride=SUBLANES)
            for slab in range(n_slab):
                xi_ref[slab, dst, :] = tile[:, slab * LANES:(slab + 1) * LANES]
        return carry

    lax.fori_loop(0, seg_len // SUBLANES, interleave, 0)

    sub = lax.broadcasted_iota(jnp.int32, (SUBLANES, LANES), 0)
    for slab in range(n_slab):
        for back in range(1, CONV_LEFT + 1):
            src = halo + (seg_len - back) * SUBLANES
            tile = pltpu.roll(xi_ref[slab, src:src + SUBLANES, :], 1, 0)
            xi_ref[slab, halo - back * SUBLANES:halo - (back - 1) * SUBLANES, :] = jnp.where(sub == 0, 0.0, tile)
        tile = pltpu.roll(xi_ref[slab, halo:halo + SUBLANES, :], SUBLANES - 1, 0)
        end = halo + seq
        xi_ref[slab, end:end + SUBLANES, :] = jnp.where(sub == SUBLANES - 1, 0.0, tile)

    cw = cw_ref[...]
    cb = cb_ref[...]

    def conv_chunk(c, carry):
        t0 = c * tch
        parts = []
        for slab in range(n_slab):
            cols = slice(slab * LANES, (slab + 1) * LANES)
            acc = cb[:, cols]
            for tap in range(CONV_WIDTH):
                r0 = pl.multiple_of((t0 + tap) * SUBLANES, SUBLANES)
                acc = acc + cw[tap:tap + 1, cols] * xi_ref[slab, pl.ds(r0, chunk_rows), :]
            parts.append(acc)
        conv_ref[pl.ds(pl.multiple_of(t0 * SUBLANES, SUBLANES), chunk_rows), :] = jnp.concatenate(parts, axis=1)
        return carry

    lax.fori_loop(0, n_chunk, conv_chunk, 0)

    def softplus(x):
        return jnp.maximum(x, 0.0) + jnp.log1p(jnp.exp(-jnp.abs(x)))

    def gates(t0, w_ref, z, a_out, b_out):
        half_conv = conv_ref[pl.ds(pl.multiple_of(t0 * SUBLANES, SUBLANES), chunk_rows), :]
        pre = jnp.dot(half_conv.astype(BF16), w_ref[...], preferred_element_type=F32)
        t_r = jnp.tanh(pre[:, :LRU_BS] + bias_ref[2 * z:2 * z + 1, :])
        t_i = jnp.tanh(pre[:, LRU_BS:] + bias_ref[2 * z + 1:2 * z + 2, :])
        c1 = (-0.5 * LRU_C) * softplus(-lam_ref[z:z + 1, :])
        a = jnp.exp(c1 * t_r + c1)
        a_out[...] = a
        v = 1.0 - a * a
        root = jnp.where(v > 0.0, v * lax.rsqrt(v), 0.0)
        b_out[...] = root * (half_conv * t_i + half_conv)

    def chunk_pair(c, carry):
        tf0 = c * tch
        tb0 = (n_chunk - 1 - c) * tch
        gates(tf0, wf_ref, 0, af_ref, bf_ref)
        gates(tb0, wb_ref, 1, ab_ref, bb_ref)

        def steps(k, state):
            hf, pf, hb, pb = state
            steps_per_trip = SUBLANES
            group = steps_per_trip * SUBLANES
            src_f = pl.ds(pl.multiple_of(k * group, group), group)
            src_b = pl.ds(pl.multiple_of(chunk_rows - group - k * group, group), group)
            dst_f = pl.ds(pl.multiple_of(tf0 * SUBLANES + k * group, group), group)
            dst_b = pl.ds(pl.multiple_of(tb0 * SUBLANES + chunk_rows - group - k * group, group), group)
            a_f, b_f = af_ref[src_f, :], bf_ref[src_f, :]
            a_b, b_b = ab_ref[src_b, :], bb_ref[src_b, :]
            hfs, pfs, hbs, pbs = [], [], [], []
            for j in range(steps_per_trip):
                up = slice(j * SUBLANES, (j + 1) * SUBLANES)
                down = slice((steps_per_trip - 1 - j) * SUBLANES, (steps_per_trip - j) * SUBLANES)
                hf = a_f[up] * hf + b_f[up]
                pf = a_f[up] * pf
                hb = a_b[down] * hb + b_b[down]
                pb = a_b[down] * pb
                hfs.append(hf)
                pfs.append(pf)
                hbs.insert(0, hb)
                pbs.insert(0, pb)
            hf_ref[dst_f, :] = jnp.concatenate(hfs, axis=0)
            pf_ref[dst_f, :] = jnp.concatenate(pfs, axis=0)
            hb_ref[dst_b, :] = jnp.concatenate(hbs, axis=0)
            pb_ref[dst_b, :] = jnp.concatenate(pbs, axis=0)
            return hf, pf, hb, pb

        return lax.fori_loop(0, tch // SUBLANES, steps, carry)

    zeros = jnp.zeros((SUBLANES, LRU_BS), F32)
    ones = jnp.ones((SUBLANES, LRU_BS), F32)
    end_f, prod_f, end_b, prod_b = lax.fori_loop(0, n_chunk, chunk_pair, (zeros, ones, zeros, ones))

    cin_f = [jnp.zeros((1, LRU_BS), F32)]
    for s in range(SUBLANES - 1):
        cin_f.append(end_f[s:s + 1, :] + prod_f[s:s + 1, :] * cin_f[s])
    cin_b = [jnp.zeros((1, LRU_BS), F32)]
    for s in range(SUBLANES - 1, 0, -1):
        cin_b.insert(0, end_b[s:s + 1, :] + prod_b[s:s + 1, :] * cin_b[0])
    cin_f_tiles = pltpu.repeat(_rows_to_tile(cin_f, reverse=False), tch, axis=0)
    cin_b_tiles = pltpu.repeat(_rows_to_tile(cin_b, reverse=False), tch, axis=0)

    def fix_up(c, carry):
        rows = pl.ds(pl.multiple_of(c * chunk_rows, chunk_rows), chunk_rows)
        total = (hf_ref[rows, :] + pf_ref[rows, :] * cin_f_tiles) + (hb_ref[rows, :] + pb_ref[rows, :] * cin_b_tiles)
        for slab in range(n_slab):
            xi_ref[slab, rows, :] = total[:, slab * LANES:(slab + 1) * LANES]
        return carry

    lax.fori_loop(0, n_chunk, fix_up, 0)

    out_rows = 2 * SUBLANES

    def finish(tb, carry):
        for s in range(SUBLANES):
            halves = []
            for half in range(2):
                t0 = tb * out_rows + half * SUBLANES
                src = pl.ds(t0 * SUBLANES + s, SUBLANES, stride=SUBLANES)
                halves.append(jnp.concatenate([xi_ref[slab, src, :] for slab in range(n_slab)], axis=1))
            rows = pl.ds(pl.multiple_of(s * seg_len + tb * out_rows, out_rows), out_rows)
            y = jnp.concatenate(halves, axis=0) * jax.nn.gelu(gate_ref[rows, :])
            o_ref[rows, :] = y.astype(o_ref.dtype)
        return carry

    lax.fori_loop(0, seg_len // out_rows, finish, 0)


def _rglru(proj, conv_w, conv_b, w_fwd, w_bwd, bias, lam, tch):
    bsz, seq, width2 = proj.shape
    n_blk = width2 // (2 * LRU_BS)
    n_slab = LRU_BS // LANES
    seq_blk = lambda off: pl.BlockSpec((None, seq, LRU_BS), lambda b, n: (b, 0, off + n))
    per_blk = lambda rows, cols: pl.BlockSpec((None, rows, cols), lambda b, n: (n, 0, 0))
    slab_rows = pltpu.VMEM((seq, LRU_BS), F32)
    gate_rows = pltpu.VMEM((tch * SUBLANES, LRU_BS), F32)
    return pl.pallas_call(
        functools.partial(_rglru_kernel, seq=seq, tch=tch),
        grid=(bsz, n_blk),
        in_specs=[
            seq_blk(0),
            seq_blk(n_blk),
            pl.BlockSpec((CONV_WIDTH, LRU_BS), lambda b, n: (0, n)),
            pl.BlockSpec((1, LRU_BS), lambda b, n: (0, n)),
            per_blk(LRU_BS, 2 * LRU_BS),
            per_blk(LRU_BS, 2 * LRU_BS),
            per_blk(4, LRU_BS),
            per_blk(2, LRU_BS),
        ],
        out_specs=pl.BlockSpec((None, seq, LRU_BS), lambda b, n: (b, 0, n)),
        out_shape=jax.ShapeDtypeStruct((bsz, seq, n_blk * LRU_BS), BF16),
        scratch_shapes=[
            pltpu.VMEM((n_slab, seq + (CONV_WIDTH - 1) * SUBLANES, LANES), F32),
            pltpu.VMEM((seq, LRU_BS), F32),
            slab_rows, slab_rows, slab_rows, slab_rows,
            gate_rows, gate_rows, gate_rows, gate_rows,
        ],
        compiler_params=_params("arbitrary", "arbitrary"),
        name="rglru",
    )(proj, proj, conv_w, conv_b.reshape(1, -1), w_fwd, w_bwd, bias, lam)


def _even_layer(x, p, bsz, seq):
    m, d = x.shape
    s5_width = p["s5_w_glu"].shape[0]
    attn_width = N_KV_HEADS * Q_PER_KV * HEAD_DIM
    kv_width = N_KV_HEADS * HEAD_DIM
    proj = _norm_matmul(x, p["norm_mix"], p["w_in"], tm=1024, tn=512)

    ut = _s5_to_groups(proj, s5_width, chunks_per_step=256)
    yt = _s5_scan(ut, p["s5_w"], p["s5_m"], p["s5_v"], p["s5_a"], n_seq=bsz)
    y = _s5_from_groups(yt, chunks_per_step=256)
    y_s5 = _glu(y, p["s5_w_glu"], p["s5_b_glu"], tm=512)

    y_attn = _attention(proj.reshape(bsz, seq, -1), p["attn_sink"], p["attn_q_norm"], p["attn_k_norm"],
                        p["rope_cos"], p["rope_sin_lo"], p["rope_sin_hi"],
                        q_col0=s5_width, k_col0=s5_width + attn_width,
                        v_col0=s5_width + attn_width + kv_width, blocks_per_step=8)
    return _proj_residual(x, [y_s5, y_attn.reshape(m, attn_width)],
                          [p["w_out"][:s5_width], p["w_out"][s5_width:]], tm=1024, tn=512)


def _odd_layer(x, p, bsz, seq):
    m, d = x.shape
    proj = _norm_matmul(x, p["norm_mix"], p["w_in"], tm=1024, tn=512)
    y = _rglru(proj.reshape(bsz, seq, -1), p["conv_w"], p["conv_b"], p["w_fwd"], p["w_bwd"],
               p["gate_bias"], p["lam"], tch=64)
    return _proj_residual(x, [y.reshape(m, -1)], [p["w_out"]], tm=1024, tn=512)


def kernel(x_prompt, x_sample, norm_mix, norm_ffn, ev_w_in, ev_w_out, s5_lam_re, s5_lam_im, s5_log_dt, s5_b_re, s5_b_im, s5_c_re, s5_c_im, s5_d, s5_w_glu, s5_b_glu, attn_q_norm, attn_k_norm, attn_sink, od_w_in, od_w_out, lru_conv_w, lru_conv_b, lru_wa, lru_ba, lru_wx, lru_bx, lru_lam, ffn_w1, ffn_w3, ffn_w2):
    depth = norm_mix.shape[0]
    seq_lens = {x_prompt.shape[1], x_sample.shape[1]}
    rope = {s: _rope_tables(s) for s in seq_lens}

    layers = []
    for layer in range(depth):
        if layer % 2 == 0:
            e = layer // 2
            s5_w, s5_m, s5_v, s5_a = _s5_operators(s5_lam_re[e], s5_lam_im[e], s5_log_dt[e], s5_b_re[e],
                                                   s5_b_im[e], s5_c_re[e], s5_c_im[e], s5_d[e])
            mix = dict(norm_mix=norm_mix[layer], w_in=ev_w_in[e].astype(BF16), w_out=ev_w_out[e].astype(BF16),
                       s5_w=s5_w, s5_m=s5_m, s5_v=s5_v, s5_a=s5_a,
                       s5_w_glu=s5_w_glu[e].astype(BF16), s5_b_glu=s5_b_glu[e],
                       attn_q_norm=attn_q_norm[e], attn_k_norm=attn_k_norm[e], attn_sink=attn_sink[e])
        else:
            o = layer // 2
            n_blk = lru_wa.shape[2]
            mix = dict(norm_mix=norm_mix[layer], w_in=od_w_in[o].astype(BF16), w_out=od_w_out[o].astype(BF16),
                       conv_w=0.5 * lru_conv_w[o], conv_b=0.5 * lru_conv_b[o],
                       w_fwd=jnp.concatenate([lru_wa[o, 0], lru_wx[o, 0]], axis=-1).astype(BF16),
                       w_bwd=jnp.concatenate([lru_wa[o, 1], lru_wx[o, 1]], axis=-1).astype(BF16),
                       gate_bias=(0.5 * jnp.stack([lru_ba[o, 0], lru_bx[o, 0], lru_ba[o, 1], lru_bx[o, 1]], axis=0))
                       .reshape(4, n_blk, LRU_BS).transpose(1, 0, 2),
                       lam=lru_lam[o].reshape(2, n_blk, LRU_BS).transpose(1, 0, 2))
        ffn = dict(gain=norm_ffn[layer], w1=ffn_w1[layer].astype(BF16), w3=ffn_w3[layer].astype(BF16),
                   w2=ffn_w2[layer].astype(BF16))
        layers.append((mix, ffn))

    def run(x):
        bsz, seq, d = x.shape
        h = x.reshape(bsz * seq, d)
        for layer, (mix, ffn) in enumerate(layers):
            if layer % 2 == 0:
                cos, sin_lo, sin_hi = rope[seq]
                h = _even_layer(h, dict(mix, rope_cos=cos, rope_sin_lo=sin_lo, rope_sin_hi=sin_hi), bsz, seq)
            else:
                h = _odd_layer(h, mix, bsz, seq)
            h = _ffn(h, ffn["gain"], ffn["w1"], ffn["w3"], ffn["w2"], tm=512, tf=512)
        return h.reshape(bsz, seq, d)

    return (run(x_prompt), run(x_sample))
```

```python
import functools
import math

import jax
import jax.numpy as jnp
from jax import lax
from jax.experimental import pallas as pl
from jax.experimental.pallas import tpu as pltpu

F32 = jnp.float32
BF16 = jnp.bfloat16

LANES = 128
SUBLANES = 8
VMEM_LIMIT_BYTES = 56 * 1024 * 1024

EPS = 1e-6
NEG_INF = -1e30
S5_GROUP = 16
S5_STATE = 64
HEAD_DIM = 128
N_KV_HEADS = 2
Q_PER_KV = 4
WINDOW = 128
BLOCK = 128
ROPE_HALF = 16
ROPE_THETA = 500000.0
LRU_BS = 256
LRU_C = 8.0
CONV_WIDTH = 4
CONV_LEFT = 2

S5_CHUNK = 16
S5_VEC = S5_CHUNK * S5_GROUP
S5_HALF = 2 * S5_STATE


def _params(*semantics):
    return pltpu.CompilerParams(dimension_semantics=semantics, vmem_limit_bytes=VMEM_LIMIT_BYTES)


def _rows_to_tile(rows, reverse):
    n = rows[0].shape[1]
    sub = lax.broadcasted_iota(jnp.int32, (SUBLANES, n), 0)
    tile = jnp.broadcast_to(rows[0], (SUBLANES, n))
    for j in range(1, SUBLANES):
        tile = jnp.where(sub == (SUBLANES - 1 - j if reverse else j), jnp.broadcast_to(rows[j], (SUBLANES, n)), tile)
    return tile


def _rms_normalize(x, gain):
    ms = jnp.mean(x * x, axis=-1, keepdims=True)
    return x * lax.rsqrt(ms + EPS) * gain


def _norm_matmul_kernel(x_ref, g_ref, w_ref, o_ref, hn_ref):
    s = pl.program_id(0)
    slot = s % 2

    @pl.when(s == 0)
    def _():
        hn_ref[1] = jnp.zeros(hn_ref.shape[1:], BF16)

    o_ref[...] = jnp.dot(hn_ref[1 - slot], w_ref[...], preferred_element_type=F32)
    hn_ref[slot] = _rms_normalize(x_ref[...], g_ref[...]).astype(BF16)


def _norm_matmul(x, gain, w, tm):
    m, d = x.shape
    n = w.shape[1]
    n_blk = m // tm
    return pl.pallas_call(
        _norm_matmul_kernel,
        grid=(n_blk + 1,),
        in_specs=[
            pl.BlockSpec((tm, d), lambda s: (jnp.minimum(s, n_blk - 1), 0)),
            pl.BlockSpec((1, d), lambda s: (0, 0)),
            pl.BlockSpec((d, n), lambda s: (0, 0), pipeline_mode=pl.Buffered(1)),
        ],
        out_specs=pl.BlockSpec((tm, n), lambda s: (jnp.maximum(s - 1, 0), 0)),
        out_shape=jax.ShapeDtypeStruct((m, n), F32),
        scratch_shapes=[pltpu.VMEM((2, tm, d), BF16)],
        compiler_params=_params("arbitrary"),
        name="norm_matmul",
    )(x, gain.reshape(1, d), w)


def _proj_residual_kernel(*refs, n_in):
    x_ref = refs[0]
    a_refs = refs[1:1 + n_in]
    w_refs = refs[1 + n_in:1 + 2 * n_in]
    o_ref = refs[1 + 2 * n_in]
    acc = x_ref[...]
    for a_ref, w_ref in zip(a_refs, w_refs):
        acc = acc + jnp.dot(a_ref[...], w_ref[...], preferred_element_type=F32)
    o_ref[...] = acc


def _proj_residual(x, acts, weights, tm, tn):
    m, n = x.shape
    n_in = len(acts)
    in_specs = [pl.BlockSpec((tm, tn), lambda i, j: (i, j))]
    in_specs += [pl.BlockSpec((tm, a.shape[1]), lambda i, j: (i, 0)) for a in acts]
    in_specs += [pl.BlockSpec((w.shape[0], tn), lambda i, j: (0, j)) for w in weights]
    return pl.pallas_call(
        functools.partial(_proj_residual_kernel, n_in=n_in),
        grid=(m // tm, n // tn),
        in_specs=in_specs,
        out_specs=pl.BlockSpec((tm, tn), lambda i, j: (i, j)),
        out_shape=jax.ShapeDtypeStruct((m, n), F32),
        compiler_params=_params("arbitrary", "arbitrary"),
        name="proj_residual",
    )(x, *acts, *weights)


def _ffn_kernel(x_ref, g_ref, w1_ref, w3_ref, w2_ref, o_ref, hn_ref):
    j = pl.program_id(1)

    @pl.when(j == 0)
    def _():
        x = x_ref[...]
        hn_ref[...] = _rms_normalize(x, g_ref[...]).astype(BF16)
        o_ref[...] = x

    hn = hn_ref[...]
    a = jnp.dot(hn, w1_ref[...], preferred_element_type=F32)
    b = jnp.dot(hn, w3_ref[...], preferred_element_type=F32)
    act = (a * jax.nn.sigmoid(a) * b).astype(BF16)
    o_ref[...] += jnp.dot(act, w2_ref[...], preferred_element_type=F32)


def _ffn(x, gain, w1, w3, w2, layer, tm, tf):
    m, d = x.shape
    f = w1.shape[2]
    return pl.pallas_call(
        _ffn_kernel,
        grid=(m // tm, f // tf),
        in_specs=[
            pl.BlockSpec((tm, d), lambda i, j: (i, 0)),
            pl.BlockSpec((1, d), lambda i, j: (0, 0)),
            pl.BlockSpec((None, d, tf), lambda i, j: (layer, 0, j)),
            pl.BlockSpec((None, d, tf), lambda i, j: (layer, 0, j)),
            pl.BlockSpec((None, tf, d), lambda i, j: (layer, j, 0)),
        ],
        out_specs=pl.BlockSpec((tm, d), lambda i, j: (i, 0)),
        out_shape=jax.ShapeDtypeStruct((m, d), F32),
        scratch_shapes=[pltpu.VMEM((tm, d), BF16)],
        compiler_params=_params("arbitrary", "arbitrary"),
        name="ffn",
    )(x, gain.reshape(1, d), w1, w3, w2)


def _glu_kernel(y_ref, w_ref, b_ref, o_ref):
    g = jax.nn.gelu(y_ref[...])
    z = jnp.dot(g.astype(BF16), w_ref[...], preferred_element_type=F32) + b_ref[...]
    o_ref[...] = (g * jax.nn.sigmoid(z)).astype(o_ref.dtype)


def _glu(y, w, b, tm):
    m, c = y.shape
    return pl.pallas_call(
        _glu_kernel,
        grid=(m // tm,),
        in_specs=[
            pl.BlockSpec((tm, c), lambda i: (i, 0)),
            pl.BlockSpec((c, c), lambda i: (0, 0)),
            pl.BlockSpec((1, c), lambda i: (0, 0)),
        ],
        out_specs=pl.BlockSpec((tm, c), lambda i: (i, 0)),
        out_shape=jax.ShapeDtypeStruct((m, c), BF16),
        compiler_params=_params("arbitrary"),
        name="s5_glu",
    )(y, w, b.reshape(1, c))


def _s5_kernel(ut_ref, w_ref, m_ref, v_ref, a_ref, yt_ref, s_ref, hf_ref, hb_ref, *, n_seq, n_chunk):
    u = ut_ref[...].T.astype(BF16)
    s = jnp.dot(u, w_ref[...], preferred_element_type=F32)
    s_ref[0] = s[:, :S5_HALF]
    s_ref[1] = s[:, S5_HALF:]
    ar = a_ref[0:1, :]
    ai = a_ref[1:2, :]
    is_fwd = lax.broadcasted_iota(jnp.int32, (1, S5_HALF), 1) < S5_STATE

    def step(i, carry):
        hre, him = carry
        rows_f = pl.ds(i, n_seq, stride=n_chunk)
        rows_b = pl.ds(n_chunk - 1 - i, n_seq, stride=n_chunk)
        hf_ref[0, rows_f, :] = hre
        hf_ref[1, rows_f, :] = him
        hb_ref[0, rows_b, :] = hre
        hb_ref[1, rows_b, :] = him
        sre = jnp.where(is_fwd, s_ref[0, rows_f, :], s_ref[0, rows_b, :])
        sim = jnp.where(is_fwd, s_ref[1, rows_f, :], s_ref[1, rows_b, :])
        return ar * hre - ai * him + sre, ar * him + ai * hre + sim

    zero = jnp.zeros((n_seq, S5_HALF), F32)
    lax.fori_loop(0, n_chunk, step, (zero, zero), unroll=8)

    h_re = jnp.where(is_fwd, hf_ref[0], hb_ref[0])
    h_im = jnp.where(is_fwd, hf_ref[1], hb_ref[1])
    h = jnp.concatenate([h_re, h_im], axis=1).astype(BF16)
    y = (jnp.dot(u, m_ref[...], preferred_element_type=F32)
         + jnp.dot(h, v_ref[...], preferred_element_type=F32))
    yt_ref[...] = y.T


def _s5_scan(ut, w, m, v, a_t, n_seq):
    groups, vec, rows = ut.shape
    n_chunk = rows // n_seq
    return pl.pallas_call(
        functools.partial(_s5_kernel, n_seq=n_seq, n_chunk=n_chunk),
        grid=(groups,),
        in_specs=[
            pl.BlockSpec((None, vec, rows), lambda g: (g, 0, 0)),
            pl.BlockSpec((None, vec, 2 * S5_HALF), lambda g: (g, 0, 0)),
            pl.BlockSpec((None, vec, vec), lambda g: (g, 0, 0)),
            pl.BlockSpec((None, 2 * S5_HALF, vec), lambda g: (g, 0, 0)),
            pl.BlockSpec((None, 2, S5_HALF), lambda g: (g, 0, 0)),
        ],
        out_specs=pl.BlockSpec((None, vec, rows), lambda g: (g, 0, 0)),
        out_shape=jax.ShapeDtypeStruct((groups, vec, rows), F32),
        scratch_shapes=[
            pltpu.VMEM((2, rows, S5_HALF), F32),
            pltpu.VMEM((2, rows, S5_HALF), F32),
            pltpu.VMEM((2, rows, S5_HALF), F32),
        ],
        compiler_params=_params("arbitrary"),
        name="s5_scan",
    )(ut, w, m, v, a_t)


GROUPS_PER_LANE_BLOCK = LANES // S5_GROUP


def _to_groups_kernel(x_ref, o_ref):
    n_chunk = o_ref.shape[2]
    for tau in range(S5_CHUNK):
        rows = x_ref[pl.ds(tau, n_chunk, stride=S5_CHUNK), :]
        o_ref[:, tau * S5_GROUP:(tau + 1) * S5_GROUP, :] = rows.T.reshape(GROUPS_PER_LANE_BLOCK, S5_GROUP, n_chunk)


def _from_groups_kernel(y_ref, o_ref):
    n_chunk = y_ref.shape[2]
    for tau in range(S5_CHUNK):
        slab = y_ref[:, tau * S5_GROUP:(tau + 1) * S5_GROUP, :].reshape(LANES, n_chunk)
        o_ref[pl.ds(tau, n_chunk, stride=S5_CHUNK), :] = slab.T


def _s5_to_groups(x, width, chunks_per_step):
    tokens = x.shape[0]
    n_chunk = tokens // S5_CHUNK
    return pl.pallas_call(
        _to_groups_kernel,
        grid=(n_chunk // chunks_per_step, width // LANES),
        in_specs=[pl.BlockSpec((chunks_per_step * S5_CHUNK, LANES), lambda i, c: (i, c))],
        out_specs=pl.BlockSpec((GROUPS_PER_LANE_BLOCK, S5_VEC, chunks_per_step), lambda i, c: (c, 0, i)),
        out_shape=jax.ShapeDtypeStruct((width // S5_GROUP, S5_VEC, n_chunk), F32),
        compiler_params=_params("arbitrary", "arbitrary"),
        name="s5_to_groups",
    )(x)


def _s5_from_groups(yt, chunks_per_step):
    groups, _, n_chunk = yt.shape
    width = groups * S5_GROUP
    return pl.pallas_call(
        _from_groups_kernel,
        grid=(n_chunk // chunks_per_step, width // LANES),
        in_specs=[pl.BlockSpec((GROUPS_PER_LANE_BLOCK, S5_VEC, chunks_per_step), lambda i, c: (c, 0, i))],
        out_specs=pl.BlockSpec((chunks_per_step * S5_CHUNK, LANES), lambda i, c: (i, c)),
        out_shape=jax.ShapeDtypeStruct((n_chunk * S5_CHUNK, width), F32),
        compiler_params=_params("arbitrary", "arbitrary"),
        name="s5_from_groups",
    )(yt)


def _s5_operators(lam_re, lam_im, log_dt, b_re, b_im, c_re, c_im, d_skip):
    hi = lax.Precision.HIGHEST
    t = S5_CHUNK
    groups = lam_re.shape[1]
    lr = jnp.minimum(lam_re.astype(F32), -1e-4)
    li = lam_im.astype(F32)
    dt = jnp.exp(log_dt.astype(F32))[..., None]
    mag = jnp.exp(lr * dt)
    ab_re = mag * jnp.cos(li * dt)
    ab_im = mag * jnp.sin(li * dt)
    nr, ni = ab_re - 1.0, ab_im
    den = lr * lr + li * li
    f_re = (nr * lr + ni * li) / den
    f_im = (ni * lr - nr * li) / den
    br, bi = b_re.astype(F32), b_im.astype(F32)
    bb_re = f_re[..., None] * br - f_im[..., None] * bi
    bb_im = f_re[..., None] * bi + f_im[..., None] * br
    cr, ci = c_re.astype(F32), c_im.astype(F32)

    k = jnp.arange(t + 1, dtype=F32)
    mag_k = jnp.exp((lr * dt)[..., None] * k)
    ang_k = (li * dt)[..., None] * k
    pr = mag_k * jnp.cos(ang_k)
    pi = mag_k * jnp.sin(ang_k)

    ab_r = pr[..., :t, None] * bb_re[..., None, :] - pi[..., :t, None] * bb_im[..., None, :]
    ab_i = pr[..., :t, None] * bb_im[..., None, :] + pi[..., :t, None] * bb_re[..., None, :]
    lag = (jnp.einsum('zgnp,zgpkm->zgknm', cr, ab_r, precision=hi)
           - jnp.einsum('zgnp,zgpkm->zgknm', ci, ab_i, precision=hi))

    no_pad = ((0, 0), (0, 0), (0, 0))
    lag_f = jnp.pad(lag[0], ((0, 0), (t, 0)) + no_pad[1:])
    lag_b = jnp.pad(lag[1][:, ::-1], ((0, 0), (0, t)) + no_pad[1:])
    fwd = jnp.stack([lag_f[:, t - s:2 * t - s] for s in range(t)], axis=1)
    bwd = jnp.stack([lag_b[:, t - 1 - s:2 * t - 1 - s] for s in range(t)], axis=1)
    skip = (jnp.eye(t, dtype=F32)[None, :, :, None, None]
            * (d_skip.astype(F32).reshape(groups, S5_GROUP)[:, None, None, :, None]
               * jnp.eye(S5_GROUP, dtype=F32)[None, None, None]))
    m_op = (fwd + bwd + skip).transpose(0, 1, 4, 2, 3).reshape(groups, S5_VEC, S5_VEC)

    pf_r, pf_i = pr[0][..., :t][..., ::-1], pi[0][..., :t][..., ::-1]
    pb_r, pb_i = pr[1][..., :t], pi[1][..., :t]

    def contrib(p_r, p_i, z):
        re = p_r[..., None] * bb_re[z][:, :, None, :] - p_i[..., None] * bb_im[z][:, :, None, :]
        im = p_r[..., None] * bb_im[z][:, :, None, :] + p_i[..., None] * bb_re[z][:, :, None, :]
        to_rows = lambda x: x.transpose(0, 2, 3, 1).reshape(groups, S5_VEC, S5_STATE)
        return to_rows(re), to_rows(im)

    wf_re, wf_im = contrib(pf_r, pf_i, 0)
    wb_re, wb_im = contrib(pb_r, pb_i, 1)
    w_op = jnp.concatenate([wf_re, wb_re, wf_im, wb_im], axis=-1)

    qf_r, qf_i = pr[0][..., 1:], pi[0][..., 1:]
    qb_r, qb_i = pr[1][..., 1:][..., ::-1], pi[1][..., 1:][..., ::-1]

    def readout(q_r, q_i, z):
        c_r = cr[z].transpose(0, 2, 1)[:, :, None, :]
        c_i = ci[z].transpose(0, 2, 1)[:, :, None, :]
        w_r = c_r * q_r[..., None] - c_i * q_i[..., None]
        w_i = c_r * q_i[..., None] + c_i * q_r[..., None]
        return w_r.reshape(groups, S5_STATE, S5_VEC), (-w_i).reshape(groups, S5_STATE, S5_VEC)

    vf_re, vf_im = readout(qf_r, qf_i, 0)
    vb_re, vb_im = readout(qb_r, qb_i, 1)
    v_op = jnp.concatenate([vf_re, vb_re, vf_im, vb_im], axis=1)

    a_t = jnp.stack([jnp.concatenate([pr[0][..., t], pr[1][..., t]], axis=-1),
                     jnp.concatenate([pi[0][..., t], pi[1][..., t]], axis=-1)], axis=1)
    return w_op.astype(BF16), m_op.astype(BF16), v_op.astype(BF16), a_t


def _rope(x, cos, sin_lo, sin_hi):
    return (x * cos
            + pltpu.roll(x, HEAD_DIM - ROPE_HALF, 1) * sin_lo
            + pltpu.roll(x, ROPE_HALF, 1) * sin_hi)


def _attn_kernel(sink_ref, q_ref, k_ref, v_ref, qn_ref, kn_ref, cos_ref, slo_ref, shi_ref,
                 o_ref, kb_ref, vb_ref, *, seq, blocks_per_step):
    h = pl.program_id(1)
    n = pl.program_id(2)
    prep_rows = 256

    @pl.when(n == 0)
    def _():
        pad = jnp.zeros((BLOCK, HEAD_DIM), BF16)
        kb_ref[0:BLOCK, :] = pad
        vb_ref[0:BLOCK, :] = pad
        kb_ref[BLOCK + seq:, :] = pad
        vb_ref[BLOCK + seq:, :] = pad

        def prep(c, carry):
            r0 = pl.multiple_of(c * prep_rows, prep_rows)
            rows = pl.ds(r0, prep_rows)
            k = _rms_normalize(k_ref[rows, :], kn_ref[...])
            k = _rope(k, cos_ref[rows, :], slo_ref[rows, :], shi_ref[rows, :])
            kb_ref[pl.ds(r0 + BLOCK, prep_rows), :] = k.astype(BF16)
            vb_ref[pl.ds(r0 + BLOCK, prep_rows), :] = v_ref[rows, :].astype(BF16)
            return carry

        lax.fori_loop(0, seq // prep_rows, prep, 0)

    qi = lax.broadcasted_iota(jnp.int32, (BLOCK, 1), 0)
    kj = lax.broadcasted_iota(jnp.int32, (BLOCK, 3 * BLOCK), 1)

    def query_block(qb, carry):
        local = pl.multiple_of(qb * BLOCK, BLOCK)
        r0 = pl.multiple_of((n * blocks_per_step + qb) * BLOCK, BLOCK)
        rows = pl.ds(r0, BLOCK)
        cos, slo, shi = cos_ref[rows, :], slo_ref[rows, :], shi_ref[rows, :]
        heads = []
        for g in range(Q_PER_KV):
            qg = _rms_normalize(q_ref[pl.ds(local, BLOCK), g * HEAD_DIM:(g + 1) * HEAD_DIM], qn_ref[...])
            heads.append(_rope(qg, cos, slo, shi).astype(BF16))
        q4 = jnp.concatenate(heads, axis=0)
        kw = kb_ref[pl.ds(r0, 3 * BLOCK), :]
        vw = vb_ref[pl.ds(r0, 3 * BLOCK), :]
        s = lax.dot_general(q4, kw, (((1,), (1,)), ((), ())), preferred_element_type=F32)
        s = s * (HEAD_DIM ** -0.5)

        lo = jnp.maximum(qi + (BLOCK - WINDOW), BLOCK - r0)
        hi = jnp.minimum(qi + (BLOCK + WINDOW), seq - 1 + BLOCK - r0)
        probs = []
        inv_denoms = []
        for g in range(Q_PER_KV):
            sg = s[g * BLOCK:(g + 1) * BLOCK, :]
            sg = jnp.where(kj >= lo, jnp.where(kj <= hi, sg, NEG_INF), NEG_INF)
            sk = sink_ref[h * Q_PER_KV + g]
            mx = jnp.maximum(jnp.max(sg, axis=-1, keepdims=True), sk)
            p = jnp.exp(sg - mx)
            denom = jnp.sum(p, axis=-1, keepdims=True) + jnp.exp(sk - mx)
            probs.append(p.astype(BF16))
            inv_denoms.append(1.0 / denom)
        p4 = jnp.concatenate(probs, axis=0)
        o4 = jnp.dot(p4, vw, preferred_element_type=F32)
        for g in range(Q_PER_KV):
            og = o4[g * BLOCK:(g + 1) * BLOCK, :] * inv_denoms[g]
            o_ref[pl.ds(local, BLOCK), g * HEAD_DIM:(g + 1) * HEAD_DIM] = og.astype(o_ref.dtype)
        return carry

    lax.fori_loop(0, blocks_per_step, query_block, 0)


def _attention(proj, sink, q_norm, k_norm, cos, sin_lo, sin_hi, q_col0, k_col0, v_col0, blocks_per_step):
    bsz, seq, _ = proj.shape
    q_w = Q_PER_KV * HEAD_DIM
    q_blk0, k_blk0, v_blk0 = q_col0 // q_w, k_col0 // HEAD_DIM, v_col0 // HEAD_DIM
    table = pl.BlockSpec((seq, HEAD_DIM), lambda b, h, n: (0, 0))
    step_rows = blocks_per_step * BLOCK
    return pl.pallas_call(
        functools.partial(_attn_kernel, seq=seq, blocks_per_step=blocks_per_step),
        grid=(bsz, N_KV_HEADS, seq // step_rows),
        in_specs=[
            pl.BlockSpec(memory_space=pltpu.SMEM),
            pl.BlockSpec((None, step_rows, q_w), lambda b, h, n: (b, n, q_blk0 + h)),
            pl.BlockSpec((None, seq, HEAD_DIM), lambda b, h, n: (b, 0, k_blk0 + h)),
            pl.BlockSpec((None, seq, HEAD_DIM), lambda b, h, n: (b, 0, v_blk0 + h)),
            pl.BlockSpec((1, HEAD_DIM), lambda b, h, n: (0, 0)),
            pl.BlockSpec((1, HEAD_DIM), lambda b, h, n: (0, 0)),
            table, table, table,
        ],
        out_specs=pl.BlockSpec((None, step_rows, q_w), lambda b, h, n: (b, n, h)),
        out_shape=jax.ShapeDtypeStruct((bsz, seq, N_KV_HEADS * q_w), BF16),
        scratch_shapes=[
            pltpu.VMEM((seq + 2 * BLOCK, HEAD_DIM), BF16),
            pltpu.VMEM((seq + 2 * BLOCK, HEAD_DIM), BF16),
        ],
        compiler_params=_params("arbitrary", "arbitrary", "arbitrary"),
        name="window_attention",
    )(sink, proj, proj, proj, q_norm.reshape(1, HEAD_DIM), k_norm.reshape(1, HEAD_DIM), cos, sin_lo, sin_hi)


def _rope_tables(seq):
    inv = ROPE_THETA ** (-jnp.arange(ROPE_HALF, dtype=F32) / ROPE_HALF)
    ang = jnp.arange(seq).astype(F32)[:, None] * inv[None, :]
    cos, sin = jnp.cos(ang), jnp.sin(ang)
    rest = HEAD_DIM - 2 * ROPE_HALF
    cos_t = jnp.concatenate([cos, cos, jnp.ones((seq, rest), F32)], axis=-1)
    sin_lo = jnp.concatenate([-sin, jnp.zeros((seq, HEAD_DIM - ROPE_HALF), F32)], axis=-1)
    sin_hi = jnp.concatenate([jnp.zeros((seq, ROPE_HALF), F32), sin, jnp.zeros((seq, rest), F32)], axis=-1)
    return cos_t, sin_lo, sin_hi


def _rglru_kernel(gate_ref, xb_ref, cw_ref, cb_ref, wf_ref, wb_ref, bias_ref, lam_ref, o_ref,
                  xi_ref, conv_ref, hf_ref, pf_ref, hb_ref, pb_ref, af_ref, bf_ref, ab_ref, bb_ref, *, seq, tch):
    seg_len = seq // SUBLANES
    n_slab = LRU_BS // LANES
    halo = CONV_LEFT * SUBLANES
    n_chunk = seg_len // tch
    chunk_rows = tch * SUBLANES

    def interleave(tb, carry):
        t0 = tb * SUBLANES
        for s in range(SUBLANES):
            tile = xb_ref[pl.ds(pl.multiple_of(s * seg_len + t0, SUBLANES), SUBLANES), :]
            dst = pl.ds(halo + t0 * SUBLANES + s, SUBLANES, stride=SUBLANES)
            for slab in range(n_slab):
                xi_ref[slab, dst, :] = tile[:, slab * LANES:(slab + 1) * LANES]
        return carry

    lax.fori_loop(0, seg_len // SUBLANES, interleave, 0)

    sub = lax.broadcasted_iota(jnp.int32, (SUBLANES, LANES), 0)
    for slab in range(n_slab):
        for back in range(1, CONV_LEFT + 1):
            src = halo + (seg_len - back) * SUBLANES
            tile = pltpu.roll(xi_ref[slab, src:src + SUBLANES, :], 1, 0)
            xi_ref[slab, halo - back * SUBLANES:halo - (back - 1) * SUBLANES, :] = jnp.where(sub == 0, 0.0, tile)
        tile = pltpu.roll(xi_ref[slab, halo:halo + SUBLANES, :], SUBLANES - 1, 0)
        end = halo + seq
        xi_ref[slab, end:end + SUBLANES, :] = jnp.where(sub == SUBLANES - 1, 0.0, tile)

    cw = cw_ref[...]
    cb = cb_ref[...]

    def conv_chunk(c, carry):
        t0 = c * tch
        parts = []
        for slab in range(n_slab):
            cols = slice(slab * LANES, (slab + 1) * LANES)
            acc = cb[:, cols]
            for tap in range(CONV_WIDTH):
                r0 = pl.multiple_of((t0 + tap) * SUBLANES, SUBLANES)
                acc = acc + cw[tap:tap + 1, cols] * xi_ref[slab, pl.ds(r0, chunk_rows), :]
            parts.append(acc)
        conv_ref[pl.ds(pl.multiple_of(t0 * SUBLANES, SUBLANES), chunk_rows), :] = jnp.concatenate(parts, axis=1)
        return carry

    lax.fori_loop(0, n_chunk, conv_chunk, 0)

    def softplus(x):
        return jnp.maximum(x, 0.0) + jnp.log1p(jnp.exp(-jnp.abs(x)))

    def gates(t0, w_ref, z, a_out, b_out):
        half_conv = conv_ref[pl.ds(pl.multiple_of(t0 * SUBLANES, SUBLANES), chunk_rows), :]
        pre = jnp.dot(half_conv.astype(BF16), w_ref[...], preferred_element_type=F32)
        t_r = jnp.tanh(pre[:, :LRU_BS] + bias_ref[2 * z:2 * z + 1, :])
        t_i = jnp.tanh(pre[:, LRU_BS:] + bias_ref[2 * z + 1:2 * z + 2, :])
        c1 = (-0.5 * LRU_C) * softplus(-lam_ref[z:z + 1, :])
        a = jnp.exp(c1 * t_r + c1)
        a_out[...] = a
        v = 1.0 - a * a
        root = jnp.where(v > 0.0, v * lax.rsqrt(v), 0.0)
        b_out[...] = root * (half_conv * t_i + half_conv)

    def chunk_pair(c, carry):
        tf0 = c * tch
        tb0 = (n_chunk - 1 - c) * tch
        gates(tf0, wf_ref, 0, af_ref, bf_ref)
        gates(tb0, wb_ref, 1, ab_ref, bb_ref)

        def steps(k, state):
            hf, pf, hb, pb = state
            steps_per_trip = SUBLANES
            group = steps_per_trip * SUBLANES
            src_f = pl.ds(pl.multiple_of(k * group, group), group)
            src_b = pl.ds(pl.multiple_of(chunk_rows - group - k * group, group), group)
            dst_f = pl.ds(pl.multiple_of(tf0 * SUBLANES + k * group, group), group)
            dst_b = pl.ds(pl.multiple_of(tb0 * SUBLANES + chunk_rows - group - k * group, group), group)
            a_f, b_f = af_ref[src_f, :], bf_ref[src_f, :]
            a_b, b_b = ab_ref[src_b, :], bb_ref[src_b, :]
            hfs, pfs, hbs, pbs = [], [], [], []
            for j in range(steps_per_trip):
                up = slice(j * SUBLANES, (j + 1) * SUBLANES)
                down = slice((steps_per_trip - 1 - j) * SUBLANES, (steps_per_trip - j) * SUBLANES)
                hf = a_f[up] * hf + b_f[up]
                pf = a_f[up] * pf
                hb = a_b[down] * hb + b_b[down]
                pb = a_b[down] * pb
                hfs.append(hf)
                pfs.append(pf)
                hbs.insert(0, hb)
                pbs.insert(0, pb)
            hf_ref[dst_f, :] = jnp.concatenate(hfs, axis=0)
            pf_ref[dst_f, :] = jnp.concatenate(pfs, axis=0)
            hb_ref[dst_b, :] = jnp.concatenate(hbs, axis=0)
            pb_ref[dst_b, :] = jnp.concatenate(pbs, axis=0)
            return hf, pf, hb, pb

        return lax.fori_loop(0, tch // SUBLANES, steps, carry)

    zeros = jnp.zeros((SUBLANES, LRU_BS), F32)
    ones = jnp.ones((SUBLANES, LRU_BS), F32)
    end_f, prod_f, end_b, prod_b = lax.fori_loop(0, n_chunk, chunk_pair, (zeros, ones, zeros, ones))

    cin_f = [jnp.zeros((1, LRU_BS), F32)]
    for s in range(SUBLANES - 1):
        cin_f.append(end_f[s:s + 1, :] + prod_f[s:s + 1, :] * cin_f[s])
    cin_b = [jnp.zeros((1, LRU_BS), F32)]
    for s in range(SUBLANES - 1, 0, -1):
        cin_b.insert(0, end_b[s:s + 1, :] + prod_b[s:s + 1, :] * cin_b[0])
    cin_f_tiles = pltpu.repeat(_rows_to_tile(cin_f, reverse=False), tch, axis=0)
    cin_b_tiles = pltpu.repeat(_rows_to_tile(cin_b, reverse=False), tch, axis=0)

    def fix_up(c, carry):
        rows = pl.ds(pl.multiple_of(c * chunk_rows, chunk_rows), chunk_rows)
        total = (hf_ref[rows, :] + pf_ref[rows, :] * cin_f_tiles) + (hb_ref[rows, :] + pb_ref[rows, :] * cin_b_tiles)
        for slab in range(n_slab):
            xi_ref[slab, rows, :] = total[:, slab * LANES:(slab + 1) * LANES]
        return carry

    lax.fori_loop(0, n_chunk, fix_up, 0)

    out_rows = 2 * SUBLANES

    def finish(tb, carry):
        for s in range(SUBLANES):
            halves = []
            for half in range(2):
                t0 = tb * out_rows + half * SUBLANES
                src = pl.ds(t0 * SUBLANES + s, SUBLANES, stride=SUBLANES)
                halves.append(jnp.concatenate([xi_ref[slab, src, :] for slab in range(n_slab)], axis=1))
            rows = pl.ds(pl.multiple_of(s * seg_len + tb * out_rows, out_rows), out_rows)
            y = jnp.concatenate(halves, axis=0) * jax.nn.gelu(gate_ref[rows, :])
            o_ref[rows, :] = y.astype(o_ref.dtype)
        return carry

    lax.fori_loop(0, seg_len // out_rows, finish, 0)


def _rglru(proj, conv_w, conv_b, w_fwd, w_bwd, bias, lam, tch):
    bsz, seq, width2 = proj.shape
    n_blk = width2 // (2 * LRU_BS)
    n_slab = LRU_BS // LANES
    seq_blk = lambda off: pl.BlockSpec((None, seq, LRU_BS), lambda b, n: (b, 0, off + n))
    per_blk = lambda rows, cols: pl.BlockSpec((None, rows, cols), lambda b, n: (n, 0, 0))
    slab_rows = pltpu.VMEM((seq, LRU_BS), F32)
    gate_rows = pltpu.VMEM((tch * SUBLANES, LRU_BS), F32)
    return pl.pallas_call(
        functools.partial(_rglru_kernel, seq=seq, tch=tch),
        grid=(bsz, n_blk),
        in_specs=[
            seq_blk(0),
            seq_blk(n_blk),
            pl.BlockSpec((CONV_WIDTH, LRU_BS), lambda b, n: (0, n)),
            pl.BlockSpec((1, LRU_BS), lambda b, n: (0, n)),
            per_blk(LRU_BS, 2 * LRU_BS),
            per_blk(LRU_BS, 2 * LRU_BS),
            per_blk(4, LRU_BS),
            per_blk(2, LRU_BS),
        ],
        out_specs=pl.BlockSpec((None, seq, LRU_BS), lambda b, n: (b, 0, n)),
        out_shape=jax.ShapeDtypeStruct((bsz, seq, n_blk * LRU_BS), BF16),
        scratch_shapes=[
            pltpu.VMEM((n_slab, seq + (CONV_WIDTH - 1) * SUBLANES, LANES), F32),
            pltpu.VMEM((seq, LRU_BS), F32),
            slab_rows, slab_rows, slab_rows, slab_rows,
            gate_rows, gate_rows, gate_rows, gate_rows,
        ],
        compiler_params=_params("arbitrary", "arbitrary"),
        name="rglru",
    )(proj, proj, conv_w, conv_b.reshape(1, -1), w_fwd, w_bwd, bias, lam)


def _even_layer(x, p, bsz, seq):
    m, d = x.shape
    s5_width = p["s5_w_glu"].shape[0]
    attn_width = N_KV_HEADS * Q_PER_KV * HEAD_DIM
    kv_width = N_KV_HEADS * HEAD_DIM
    proj = _norm_matmul(x, p["norm_mix"], p["w_in"], tm=512)

    ut = _s5_to_groups(proj, s5_width, chunks_per_step=256)
    yt = _s5_scan(ut, p["s5_w"], p["s5_m"], p["s5_v"], p["s5_a"], n_seq=bsz)
    y = _s5_from_groups(yt, chunks_per_step=256)
    y_s5 = _glu(y, p["s5_w_glu"], p["s5_b_glu"], tm=512)

    y_attn = _attention(proj.reshape(bsz, seq, -1), p["attn_sink"], p["attn_q_norm"], p["attn_k_norm"],
                        p["rope_cos"], p["rope_sin_lo"], p["rope_sin_hi"],
                        q_col0=s5_width, k_col0=s5_width + attn_width,
                        v_col0=s5_width + attn_width + kv_width, blocks_per_step=8)
    return _proj_residual(x, [y_s5, y_attn.reshape(m, attn_width)],
                          [p["w_out"][:s5_width], p["w_out"][s5_width:]], tm=1024, tn=512)


def _odd_layer(x, p, bsz, seq):
    m, d = x.shape
    proj = _norm_matmul(x, p["norm_mix"], p["w_in"], tm=512)
    y = _rglru(proj.reshape(bsz, seq, -1), p["conv_w"], p["conv_b"], p["w_fwd"], p["w_bwd"],
               p["gate_bias"], p["lam"], tch=64)
    return _proj_residual(x, [y.reshape(m, -1)], [p["w_out"]], tm=1024, tn=512)


def kernel(x_prompt, x_sample, norm_mix, norm_ffn, ev_w_in, ev_w_out, s5_lam_re, s5_lam_im, s5_log_dt, s5_b_re, s5_b_im, s5_c_re, s5_c_im, s5_d, s5_w_glu, s5_b_glu, attn_q_norm, attn_k_norm, attn_sink, od_w_in, od_w_out, lru_conv_w, lru_conv_b, lru_wa, lru_ba, lru_wx, lru_bx, lru_lam, ffn_w1, ffn_w3, ffn_w2):
    depth = norm_mix.shape[0]
    seq_lens = {x_prompt.shape[1], x_sample.shape[1]}
    rope = {s: _rope_tables(s) for s in seq_lens}

    layers = []
    for layer in range(depth):
        if layer % 2 == 0:
            e = layer // 2
            s5_w, s5_m, s5_v, s5_a = _s5_operators(s5_lam_re[e], s5_lam_im[e], s5_log_dt[e], s5_b_re[e],
                                                   s5_b_im[e], s5_c_re[e], s5_c_im[e], s5_d[e])
            mix = dict(norm_mix=norm_mix[layer], w_in=ev_w_in[e].astype(BF16), w_out=ev_w_out[e].astype(BF16),
                       s5_w=s5_w, s5_m=s5_m, s5_v=s5_v, s5_a=s5_a,
                       s5_w_glu=s5_w_glu[e].astype(BF16), s5_b_glu=s5_b_glu[e],
                       attn_q_norm=attn_q_norm[e], attn_k_norm=attn_k_norm[e], attn_sink=attn_sink[e])
        else:
            o = layer // 2
            n_blk = lru_wa.shape[2]
            mix = dict(norm_mix=norm_mix[layer], w_in=od_w_in[o].astype(BF16), w_out=od_w_out[o].astype(BF16),
                       conv_w=0.5 * lru_conv_w[o], conv_b=0.5 * lru_conv_b[o],
                       w_fwd=jnp.concatenate([lru_wa[o, 0], lru_wx[o, 0]], axis=-1).astype(BF16),
                       w_bwd=jnp.concatenate([lru_wa[o, 1], lru_wx[o, 1]], axis=-1).astype(BF16),
                       gate_bias=(0.5 * jnp.stack([lru_ba[o, 0], lru_bx[o, 0], lru_ba[o, 1], lru_bx[o, 1]], axis=0))
                       .reshape(4, n_blk, LRU_BS).transpose(1, 0, 2),
                       lam=lru_lam[o].reshape(2, n_blk, LRU_BS).transpose(1, 0, 2))
        layers.append(mix)
    w1_all, w3_all, w2_all = ffn_w1.astype(BF16), ffn_w3.astype(BF16), ffn_w2.astype(BF16)

    def run(x):
        bsz, seq, d = x.shape
        h = x.reshape(bsz * seq, d)
        for layer, mix in enumerate(layers):
            if layer % 2 == 0:
                cos, sin_lo, sin_hi = rope[seq]
                h = _even_layer(h, dict(mix, rope_cos=cos, rope_sin_lo=sin_lo, rope_sin_hi=sin_hi), bsz, seq)
            else:
                h = _odd_layer(h, mix, bsz, seq)
            h = _ffn(h, norm_ffn[layer], w1_all, w3_all, w2_all, layer, tm=512, tf=512)
        return h.reshape(bsz, seq, d)

    return (run(x_prompt), run(x_sample))
```

```python
import functools
import math

import jax
import jax.numpy as jnp
import numpy as np
from jax import lax
from jax.experimental import pallas as pl
from jax.experimental.pallas import tpu as pltpu

F32 = jnp.float32
BF16 = jnp.bfloat16

LANES = 128
SUBLANES = 8
VMEM_LIMIT_BYTES = 56 * 1024 * 1024

EPS = 1e-6
NEG_INF = -1e30
S5_GROUP = 16
S5_STATE = 64
HEAD_DIM = 128
N_KV_HEADS = 2
Q_PER_KV = 4
WINDOW = 128
BLOCK = 128
ROPE_HALF = 16
ROPE_THETA = 500000.0
LRU_BS = 256
LRU_C = 8.0
CONV_WIDTH = 4
CONV_LEFT = 2

S5_CHUNK = 16
S5_VEC = S5_CHUNK * S5_GROUP
S5_HALF = 2 * S5_STATE


def _params(*semantics):
    return pltpu.CompilerParams(dimension_semantics=semantics, vmem_limit_bytes=VMEM_LIMIT_BYTES)


def _rows_to_tile(rows, reverse):
    n = rows[0].shape[1]
    sub = lax.broadcasted_iota(jnp.int32, (SUBLANES, n), 0)
    tile = jnp.broadcast_to(rows[0], (SUBLANES, n))
    for j in range(1, SUBLANES):
        tile = jnp.where(sub == (SUBLANES - 1 - j if reverse else j), jnp.broadcast_to(rows[j], (SUBLANES, n)), tile)
    return tile


def _rms_normalize(x, gain):
    ms = jnp.mean(x * x, axis=-1, keepdims=True)
    return x * lax.rsqrt(ms + EPS) * gain


def _norm_matmul_kernel(x_ref, g_ref, w_ref, o_ref, hn_ref):
    s = pl.program_id(0)
    slot = s % 2

    @pl.when(s == 0)
    def _():
        hn_ref[1] = jnp.zeros(hn_ref.shape[1:], BF16)

    o_ref[...] = jnp.dot(hn_ref[1 - slot], w_ref[...], preferred_element_type=F32)
    hn_ref[slot] = _rms_normalize(x_ref[...], g_ref[...]).astype(BF16)


def _norm_matmul(x, gain, w, tm):
    m, d = x.shape
    n = w.shape[1]
    n_blk = m // tm
    return pl.pallas_call(
        _norm_matmul_kernel,
        grid=(n_blk + 1,),
        in_specs=[
            pl.BlockSpec((tm, d), lambda s: (jnp.minimum(s, n_blk - 1), 0)),
            pl.BlockSpec((1, d), lambda s: (0, 0)),
            pl.BlockSpec((d, n), lambda s: (0, 0), pipeline_mode=pl.Buffered(1)),
        ],
        out_specs=pl.BlockSpec((tm, n), lambda s: (jnp.maximum(s - 1, 0), 0)),
        out_shape=jax.ShapeDtypeStruct((m, n), F32),
        scratch_shapes=[pltpu.VMEM((2, tm, d), BF16)],
        compiler_params=_params("arbitrary"),
        name="norm_matmul",
    )(x, gain.reshape(1, d), w)


def _mix_ffn_kernel(*refs, n_in):
    x_ref = refs[0]
    a_refs = refs[1:1 + n_in]
    wo_refs = refs[1 + n_in:1 + 2 * n_in]
    g_ref, w1_ref, w3_ref, w2_ref, o_ref, hn_ref = refs[1 + 2 * n_in:]
    j = pl.program_id(1)

    @pl.when(j == 0)
    def _():
        x = x_ref[...]
        for a_ref, wo_ref in zip(a_refs, wo_refs):
            x = x + jnp.dot(a_ref[...], wo_ref[...], preferred_element_type=F32)
        hn_ref[...] = _rms_normalize(x, g_ref[...]).astype(BF16)
        o_ref[...] = x

    hn = hn_ref[...]
    a = jnp.dot(hn, w1_ref[...], preferred_element_type=F32)
    b = jnp.dot(hn, w3_ref[...], preferred_element_type=F32)
    act = (a * jax.nn.sigmoid(a) * b).astype(BF16)
    o_ref[...] += jnp.dot(act, w2_ref[...], preferred_element_type=F32)


def _mix_ffn(x, acts, wos, gain, w1, w3, w2, layer, tm, tf):
    m, d = x.shape
    f = w1.shape[2]
    n_in = len(acts)
    in_specs = [pl.BlockSpec((tm, d), lambda i, j: (i, 0))]
    in_specs += [pl.BlockSpec((tm, a.shape[1]), lambda i, j: (i, 0)) for a in acts]
    in_specs += [pl.BlockSpec(wo.shape, lambda i, j: (0, 0), pipeline_mode=pl.Buffered(1)) for wo in wos]
    in_specs += [
        pl.BlockSpec((1, d), lambda i, j: (0, 0)),
        pl.BlockSpec((None, d, tf), lambda i, j: (layer, 0, j)),
        pl.BlockSpec((None, d, tf), lambda i, j: (layer, 0, j)),
        pl.BlockSpec((None, tf, d), lambda i, j: (layer, j, 0)),
    ]
    return pl.pallas_call(
        functools.partial(_mix_ffn_kernel, n_in=n_in),
        grid=(m // tm, f // tf),
        in_specs=in_specs,
        out_specs=pl.BlockSpec((tm, d), lambda i, j: (i, 0)),
        out_shape=jax.ShapeDtypeStruct((m, d), F32),
        scratch_shapes=[pltpu.VMEM((tm, d), BF16)],
        compiler_params=_params("arbitrary", "arbitrary"),
        name="mix_ffn",
    )(x, *acts, *wos, gain.reshape(1, d), w1, w3, w2)


def _glu_kernel(y_ref, w_ref, b_ref, o_ref):
    g = jax.nn.gelu(y_ref[...])
    z = jnp.dot(g.astype(BF16), w_ref[...], preferred_element_type=F32) + b_ref[...]
    o_ref[...] = (g * jax.nn.sigmoid(z)).astype(o_ref.dtype)


def _glu(y, w, b, tm):
    m, c = y.shape
    return pl.pallas_call(
        _glu_kernel,
        grid=(m // tm,),
        in_specs=[
            pl.BlockSpec((tm, c), lambda i: (i, 0)),
            pl.BlockSpec((c, c), lambda i: (0, 0)),
            pl.BlockSpec((1, c), lambda i: (0, 0)),
        ],
        out_specs=pl.BlockSpec((tm, c), lambda i: (i, 0)),
        out_shape=jax.ShapeDtypeStruct((m, c), BF16),
        compiler_params=_params("arbitrary"),
        name="s5_glu",
    )(y, w, b.reshape(1, c))


def _s5_kernel(ut_ref, w_ref, m_ref, v_ref, a_ref, yt_ref, s_ref, hf_ref, hb_ref, *, n_seq, n_chunk):
    u = ut_ref[...].T.astype(BF16)
    s = jnp.dot(u, w_ref[...], preferred_element_type=F32)
    s_ref[0] = s[:, :S5_HALF]
    s_ref[1] = s[:, S5_HALF:]
    ar = a_ref[0:1, :]
    ai = a_ref[1:2, :]
    is_fwd = lax.broadcasted_iota(jnp.int32, (1, S5_HALF), 1) < S5_STATE

    def step(i, carry):
        hre, him = carry
        rows_f = pl.ds(i, n_seq, stride=n_chunk)
        rows_b = pl.ds(n_chunk - 1 - i, n_seq, stride=n_chunk)
        hf_ref[0, rows_f, :] = hre
        hf_ref[1, rows_f, :] = him
        hb_ref[0, rows_b, :] = hre
        hb_ref[1, rows_b, :] = him
        sre = jnp.where(is_fwd, s_ref[0, rows_f, :], s_ref[0, rows_b, :])
        sim = jnp.where(is_fwd, s_ref[1, rows_f, :], s_ref[1, rows_b, :])
        return ar * hre - ai * him + sre, ar * him + ai * hre + sim

    zero = jnp.zeros((n_seq, S5_HALF), F32)
    lax.fori_loop(0, n_chunk, step, (zero, zero), unroll=8)

    h_re = jnp.where(is_fwd, hf_ref[0], hb_ref[0])
    h_im = jnp.where(is_fwd, hf_ref[1], hb_ref[1])
    h = jnp.concatenate([h_re, h_im], axis=1).astype(BF16)
    y = (jnp.dot(u, m_ref[...], preferred_element_type=F32)
         + jnp.dot(h, v_ref[...], preferred_element_type=F32))
    yt_ref[...] = y.T


def _s5_scan(ut, w, m, v, a_t, n_seq):
    groups, vec, rows = ut.shape
    n_chunk = rows // n_seq
    return pl.pallas_call(
        functools.partial(_s5_kernel, n_seq=n_seq, n_chunk=n_chunk),
        grid=(groups,),
        in_specs=[
            pl.BlockSpec((None, vec, rows), lambda g: (g, 0, 0)),
            pl.BlockSpec((None, vec, 2 * S5_HALF), lambda g: (g, 0, 0)),
            pl.BlockSpec((None, vec, vec), lambda g: (g, 0, 0)),
            pl.BlockSpec((None, 2 * S5_HALF, vec), lambda g: (g, 0, 0)),
            pl.BlockSpec((None, 2, S5_HALF), lambda g: (g, 0, 0)),
        ],
        out_specs=pl.BlockSpec((None, vec, rows), lambda g: (g, 0, 0)),
        out_shape=jax.ShapeDtypeStruct((groups, vec, rows), F32),
        scratch_shapes=[
            pltpu.VMEM((2, rows, S5_HALF), F32),
            pltpu.VMEM((2, rows, S5_HALF), F32),
            pltpu.VMEM((2, rows, S5_HALF), F32),
        ],
        compiler_params=_params("arbitrary"),
        name="s5_scan",
    )(ut, w, m, v, a_t)


GROUPS_PER_LANE_BLOCK = LANES // S5_GROUP


def _to_groups_kernel(x_ref, o_ref):
    n_chunk = o_ref.shape[2]
    for tau in range(S5_CHUNK):
        rows = x_ref[pl.ds(tau, n_chunk, stride=S5_CHUNK), :]
        o_ref[:, tau * S5_GROUP:(tau + 1) * S5_GROUP, :] = rows.T.reshape(GROUPS_PER_LANE_BLOCK, S5_GROUP, n_chunk)


def _from_groups_kernel(y_ref, o_ref):
    n_chunk = y_ref.shape[2]
    for tau in range(S5_CHUNK):
        slab = y_ref[:, tau * S5_GROUP:(tau + 1) * S5_GROUP, :].reshape(LANES, n_chunk)
        o_ref[pl.ds(tau, n_chunk, stride=S5_CHUNK), :] = slab.T


def _s5_to_groups(x, width, chunks_per_step):
    tokens = x.shape[0]
    n_chunk = tokens // S5_CHUNK
    return pl.pallas_call(
        _to_groups_kernel,
        grid=(n_chunk // chunks_per_step, width // LANES),
        in_specs=[pl.BlockSpec((chunks_per_step * S5_CHUNK, LANES), lambda i, c: (i, c))],
        out_specs=pl.BlockSpec((GROUPS_PER_LANE_BLOCK, S5_VEC, chunks_per_step), lambda i, c: (c, 0, i)),
        out_shape=jax.ShapeDtypeStruct((width // S5_GROUP, S5_VEC, n_chunk), F32),
        compiler_params=_params("arbitrary", "arbitrary"),
        name="s5_to_groups",
    )(x)


def _s5_from_groups(yt, chunks_per_step):
    groups, _, n_chunk = yt.shape
    width = groups * S5_GROUP
    return pl.pallas_call(
        _from_groups_kernel,
        grid=(n_chunk // chunks_per_step, width // LANES),
        in_specs=[pl.BlockSpec((GROUPS_PER_LANE_BLOCK, S5_VEC, chunks_per_step), lambda i, c: (c, 0, i))],
        out_specs=pl.BlockSpec((chunks_per_step * S5_CHUNK, LANES), lambda i, c: (i, c)),
        out_shape=jax.ShapeDtypeStruct((n_chunk * S5_CHUNK, width), F32),
        compiler_params=_params("arbitrary", "arbitrary"),
        name="s5_from_groups",
    )(yt)


def _s5_operators(lam_re, lam_im, log_dt, b_re, b_im, c_re, c_im, d_skip):
    hi = lax.Precision.HIGHEST
    t = S5_CHUNK
    groups = lam_re.shape[1]
    lr = jnp.minimum(lam_re.astype(F32), -1e-4)
    li = lam_im.astype(F32)
    dt = jnp.exp(log_dt.astype(F32))[..., None]
    mag = jnp.exp(lr * dt)
    ab_re = mag * jnp.cos(li * dt)
    ab_im = mag * jnp.sin(li * dt)
    nr, ni = ab_re - 1.0, ab_im
    den = lr * lr + li * li
    f_re = (nr * lr + ni * li) / den
    f_im = (ni * lr - nr * li) / den
    br, bi = b_re.astype(F32), b_im.astype(F32)
    bb_re = f_re[..., None] * br - f_im[..., None] * bi
    bb_im = f_re[..., None] * bi + f_im[..., None] * br
    cr, ci = c_re.astype(F32), c_im.astype(F32)

    k = jnp.arange(t + 1, dtype=F32)
    mag_k = jnp.exp((lr * dt)[..., None] * k)
    ang_k = (li * dt)[..., None] * k
    pr = mag_k * jnp.cos(ang_k)
    pi = mag_k * jnp.sin(ang_k)

    ab_r = pr[..., :t, None] * bb_re[..., None, :] - pi[..., :t, None] * bb_im[..., None, :]
    ab_i = pr[..., :t, None] * bb_im[..., None, :] + pi[..., :t, None] * bb_re[..., None, :]
    lag = (jnp.einsum('zgnp,zgpkm->zgknm', cr, ab_r, precision=hi)
           - jnp.einsum('zgnp,zgpkm->zgknm', ci, ab_i, precision=hi))

    k_idx = np.arange(t)[:, None, None]
    s_idx = np.arange(t)[None, :, None]
    t_idx = np.arange(t)[None, None, :]
    sel_f = jnp.asarray((t_idx - s_idx == k_idx).astype(np.float32))
    sel_b = jnp.asarray((s_idx - t_idx == k_idx).astype(np.float32))
    skip = jnp.einsum('st,gn,nm->gsmtn', jnp.eye(t, dtype=F32), d_skip.astype(F32).reshape(groups, S5_GROUP),
                      jnp.eye(S5_GROUP, dtype=F32), precision=hi)
    m_op = (jnp.einsum('kst,gknm->gsmtn', sel_f, lag[0], precision=hi)
            + jnp.einsum('kst,gknm->gsmtn', sel_b, lag[1], precision=hi)
            + skip).reshape(groups, S5_VEC, S5_VEC)

    pf_r, pf_i = pr[0][..., :t][..., ::-1], pi[0][..., :t][..., ::-1]
    pb_r, pb_i = pr[1][..., :t], pi[1][..., :t]

    def contrib(p_r, p_i, z):
        re = p_r[..., None] * bb_re[z][:, :, None, :] - p_i[..., None] * bb_im[z][:, :, None, :]
        im = p_r[..., None] * bb_im[z][:, :, None, :] + p_i[..., None] * bb_re[z][:, :, None, :]
        to_rows = lambda x: x.transpose(0, 2, 3, 1).reshape(groups, S5_VEC, S5_STATE)
        return to_rows(re), to_rows(im)

    wf_re, wf_im = contrib(pf_r, pf_i, 0)
    wb_re, wb_im = contrib(pb_r, pb_i, 1)
    w_op = jnp.concatenate([wf_re, wb_re, wf_im, wb_im], axis=-1)

    qf_r, qf_i = pr[0][..., 1:], pi[0][..., 1:]
    qb_r, qb_i = pr[1][..., 1:][..., ::-1], pi[1][..., 1:][..., ::-1]

    def readout(q_r, q_i, z):
        c_r = cr[z].transpose(0, 2, 1)[:, :, None, :]
        c_i = ci[z].transpose(0, 2, 1)[:, :, None, :]
        w_r = c_r * q_r[..., None] - c_i * q_i[..., None]
        w_i = c_r * q_i[..., None] + c_i * q_r[..., None]
        return w_r.reshape(groups, S5_STATE, S5_VEC), (-w_i).reshape(groups, S5_STATE, S5_VEC)

    vf_re, vf_im = readout(qf_r, qf_i, 0)
    vb_re, vb_im = readout(qb_r, qb_i, 1)
    v_op = jnp.concatenate([vf_re, vb_re, vf_im, vb_im], axis=1)

    a_t = jnp.stack([jnp.concatenate([pr[0][..., t], pr[1][..., t]], axis=-1),
                     jnp.concatenate([pi[0][..., t], pi[1][..., t]], axis=-1)], axis=1)
    return w_op.astype(BF16), m_op.astype(BF16), v_op.astype(BF16), a_t


def _rope(x, cos, sin_lo, sin_hi):
    return (x * cos
            + pltpu.roll(x, HEAD_DIM - ROPE_HALF, 1) * sin_lo
            + pltpu.roll(x, ROPE_HALF, 1) * sin_hi)


def _attn_kernel(sink_ref, q_ref, k_ref, v_ref, qn_ref, kn_ref, cos_ref, slo_ref, shi_ref,
                 o_ref, kb_ref, vb_ref, *, seq, blocks_per_step):
    h = pl.program_id(1)
    n = pl.program_id(2)
    prep_rows = 256

    @pl.when(n == 0)
    def _():
        pad = jnp.zeros((BLOCK, HEAD_DIM), BF16)
        kb_ref[0:BLOCK, :] = pad
        vb_ref[0:BLOCK, :] = pad
        kb_ref[BLOCK + seq:, :] = pad
        vb_ref[BLOCK + seq:, :] = pad

        def prep(c, carry):
            r0 = pl.multiple_of(c * prep_rows, prep_rows)
            rows = pl.ds(r0, prep_rows)
            k = _rms_normalize(k_ref[rows, :], kn_ref[...])
            k = _rope(k, cos_ref[rows, :], slo_ref[rows, :], shi_ref[rows, :])
            kb_ref[pl.ds(r0 + BLOCK, prep_rows), :] = k.astype(BF16)
            vb_ref[pl.ds(r0 + BLOCK, prep_rows), :] = v_ref[rows, :].astype(BF16)
            return carry

        lax.fori_loop(0, seq // prep_rows, prep, 0)

    qi = lax.broadcasted_iota(jnp.int32, (BLOCK, 1), 0)
    kj = lax.broadcasted_iota(jnp.int32, (BLOCK, 3 * BLOCK), 1)

    def query_block(qb, carry):
        local = pl.multiple_of(qb * BLOCK, BLOCK)
        r0 = pl.multiple_of((n * blocks_per_step + qb) * BLOCK, BLOCK)
        rows = pl.ds(r0, BLOCK)
        cos, slo, shi = cos_ref[rows, :], slo_ref[rows, :], shi_ref[rows, :]
        heads = []
        for g in range(Q_PER_KV):
            qg = _rms_normalize(q_ref[pl.ds(local, BLOCK), g * HEAD_DIM:(g + 1) * HEAD_DIM], qn_ref[...])
            heads.append(_rope(qg, cos, slo, shi).astype(BF16))
        q4 = jnp.concatenate(heads, axis=0)
        kw = kb_ref[pl.ds(r0, 3 * BLOCK), :]
        vw = vb_ref[pl.ds(r0, 3 * BLOCK), :]
        s = lax.dot_general(q4, kw, (((1,), (1,)), ((), ())), preferred_element_type=F32)
        s = s * (HEAD_DIM ** -0.5)

        lo = jnp.maximum(qi + (BLOCK - WINDOW), BLOCK - r0)
        hi = jnp.minimum(qi + (BLOCK + WINDOW), seq - 1 + BLOCK - r0)
        probs = []
        inv_denoms = []
        for g in range(Q_PER_KV):
            sg = s[g * BLOCK:(g + 1) * BLOCK, :]
            sg = jnp.where(kj >= lo, jnp.where(kj <= hi, sg, NEG_INF), NEG_INF)
            sk = sink_ref[h * Q_PER_KV + g]
            mx = jnp.maximum(jnp.max(sg, axis=-1, keepdims=True), sk)
            p = jnp.exp(sg - mx)
            denom = jnp.sum(p, axis=-1, keepdims=True) + jnp.exp(sk - mx)
            probs.append(p.astype(BF16))
            inv_denoms.append(1.0 / denom)
        p4 = jnp.concatenate(probs, axis=0)
        o4 = jnp.dot(p4, vw, preferred_element_type=F32)
        for g in range(Q_PER_KV):
            og = o4[g * BLOCK:(g + 1) * BLOCK, :] * inv_denoms[g]
            o_ref[pl.ds(local, BLOCK), g * HEAD_DIM:(g + 1) * HEAD_DIM] = og.astype(o_ref.dtype)
        return carry

    lax.fori_loop(0, blocks_per_step, query_block, 0)


def _attention(proj, sink, q_norm, k_norm, cos, sin_lo, sin_hi, q_col0, k_col0, v_col0, blocks_per_step):
    bsz, seq, _ = proj.shape
    q_w = Q_PER_KV * HEAD_DIM
    q_blk0, k_blk0, v_blk0 = q_col0 // q_w, k_col0 // HEAD_DIM, v_col0 // HEAD_DIM
    table = pl.BlockSpec((seq, HEAD_DIM), lambda b, h, n: (0, 0))
    step_rows = blocks_per_step * BLOCK
    return pl.pallas_call(
        functools.partial(_attn_kernel, seq=seq, blocks_per_step=blocks_per_step),
        grid=(bsz, N_KV_HEADS, seq // step_rows),
        in_specs=[
            pl.BlockSpec(memory_space=pltpu.SMEM),
            pl.BlockSpec((None, step_rows, q_w), lambda b, h, n: (b, n, q_blk0 + h)),
            pl.BlockSpec((None, seq, HEAD_DIM), lambda b, h, n: (b, 0, k_blk0 + h)),
            pl.BlockSpec((None, seq, HEAD_DIM), lambda b, h, n: (b, 0, v_blk0 + h)),
            pl.BlockSpec((1, HEAD_DIM), lambda b, h, n: (0, 0)),
            pl.BlockSpec((1, HEAD_DIM), lambda b, h, n: (0, 0)),
            table, table, table,
        ],
        out_specs=pl.BlockSpec((None, step_rows, q_w), lambda b, h, n: (b, n, h)),
        out_shape=jax.ShapeDtypeStruct((bsz, seq, N_KV_HEADS * q_w), BF16),
        scratch_shapes=[
            pltpu.VMEM((seq + 2 * BLOCK, HEAD_DIM), BF16),
            pltpu.VMEM((seq + 2 * BLOCK, HEAD_DIM), BF16),
        ],
        compiler_params=_params("arbitrary", "arbitrary", "arbitrary"),
        name="window_attention",
    )(sink, proj, proj, proj, q_norm.reshape(1, HEAD_DIM), k_norm.reshape(1, HEAD_DIM), cos, sin_lo, sin_hi)


def _rope_tables(seq):
    inv = ROPE_THETA ** (-jnp.arange(ROPE_HALF, dtype=F32) / ROPE_HALF)
    ang = jnp.arange(seq).astype(F32)[:, None] * inv[None, :]
    cos, sin = jnp.cos(ang), jnp.sin(ang)
    rest = HEAD_DIM - 2 * ROPE_HALF
    cos_t = jnp.concatenate([cos, cos, jnp.ones((seq, rest), F32)], axis=-1)
    sin_lo = jnp.concatenate([-sin, jnp.zeros((seq, HEAD_DIM - ROPE_HALF), F32)], axis=-1)
    sin_hi = jnp.concatenate([jnp.zeros((seq, ROPE_HALF), F32), sin, jnp.zeros((seq, rest), F32)], axis=-1)
    return cos_t, sin_lo, sin_hi


def _rglru_kernel(gate_ref, xb_ref, cw_ref, cb_ref, wf_ref, wb_ref, bias_ref, lam_ref, o_ref,
                  xi_ref, conv_ref, hf_ref, pf_ref, hb_ref, pb_ref, af_ref, bf_ref, ab_ref, bb_ref, *, seq, tch):
    seg_len = seq // SUBLANES
    n_slab = LRU_BS // LANES
    halo = CONV_LEFT * SUBLANES
    n_chunk = seg_len // tch
    chunk_rows = tch * SUBLANES

    def interleave(tb, carry):
        t0 = tb * SUBLANES
        for s in range(SUBLANES):
            tile = xb_ref[pl.ds(pl.multiple_of(s * seg_len + t0, SUBLANES), SUBLANES), :]
            dst = pl.ds(halo + t0 * SUBLANES + s, SUBLANES, stride=SUBLANES)
            for slab in range(n_slab):
                xi_ref[slab, dst, :] = tile[:, slab * LANES:(slab + 1) * LANES]
        return carry

    lax.fori_loop(0, seg_len // SUBLANES, interleave, 0)

    sub = lax.broadcasted_iota(jnp.int32, (SUBLANES, LANES), 0)
    for slab in range(n_slab):
        for back in range(1, CONV_LEFT + 1):
            src = halo + (seg_len - back) * SUBLANES
            tile = pltpu.roll(xi_ref[slab, src:src + SUBLANES, :], 1, 0)
            xi_ref[slab, halo - back * SUBLANES:halo - (back - 1) * SUBLANES, :] = jnp.where(sub == 0, 0.0, tile)
        tile = pltpu.roll(xi_ref[slab, halo:halo + SUBLANES, :], SUBLANES - 1, 0)
        end = halo + seq
        xi_ref[slab, end:end + SUBLANES, :] = jnp.where(sub == SUBLANES - 1, 0.0, tile)

    cw = cw_ref[...]
    cb = cb_ref[...]

    def conv_chunk(c, carry):
        t0 = c * tch
        parts = []
        for slab in range(n_slab):
            cols = slice(slab * LANES, (slab + 1) * LANES)
            acc = cb[:, cols]
            for tap in range(CONV_WIDTH):
                r0 = pl.multiple_of((t0 + tap) * SUBLANES, SUBLANES)
                acc = acc + cw[tap:tap + 1, cols] * xi_ref[slab, pl.ds(r0, chunk_rows), :]
            parts.append(acc)
        conv_ref[pl.ds(pl.multiple_of(t0 * SUBLANES, SUBLANES), chunk_rows), :] = jnp.concatenate(parts, axis=1)
        return carry

    lax.fori_loop(0, n_chunk, conv_chunk, 0)

    def softplus(x):
        return jnp.maximum(x, 0.0) + jnp.log1p(jnp.exp(-jnp.abs(x)))

    def gates(t0, w_ref, z, a_out, b_out):
        half_conv = conv_ref[pl.ds(pl.multiple_of(t0 * SUBLANES, SUBLANES), chunk_rows), :]
        pre = jnp.dot(half_conv.astype(BF16), w_ref[...], preferred_element_type=F32)
        t_r = jnp.tanh(pre[:, :LRU_BS] + bias_ref[2 * z:2 * z + 1, :])
        t_i = jnp.tanh(pre[:, LRU_BS:] + bias_ref[2 * z + 1:2 * z + 2, :])
        c1 = (-0.5 * LRU_C) * softplus(-lam_ref[z:z + 1, :])
        a = jnp.exp(c1 * t_r + c1)
        a_out[...] = a
        v = 1.0 - a * a
        root = jnp.where(v > 0.0, v * lax.rsqrt(v), 0.0)
        b_out[...] = root * (half_conv * t_i + half_conv)

    def chunk_pair(c, carry):
        tf0 = c * tch
        tb0 = (n_chunk - 1 - c) * tch
        gates(tf0, wf_ref, 0, af_ref, bf_ref)
        gates(tb0, wb_ref, 1, ab_ref, bb_ref)

        def steps(k, state):
            hf, pf, hb, pb = state
            steps_per_trip = SUBLANES
            group = steps_per_trip * SUBLANES
            src_f = pl.ds(pl.multiple_of(k * group, group), group)
            src_b = pl.ds(pl.multiple_of(chunk_rows - group - k * group, group), group)
            dst_f = pl.ds(pl.multiple_of(tf0 * SUBLANES + k * group, group), group)
            dst_b = pl.ds(pl.multiple_of(tb0 * SUBLANES + chunk_rows - group - k * group, group), group)
            a_f, b_f = af_ref[src_f, :], bf_ref[src_f, :]
            a_b, b_b = ab_ref[src_b, :], bb_ref[src_b, :]
            hfs, pfs, hbs, pbs = [], [], [], []
            for j in range(steps_per_trip):
                up = slice(j * SUBLANES, (j + 1) * SUBLANES)
                down = slice((steps_per_trip - 1 - j) * SUBLANES, (steps_per_trip - j) * SUBLANES)
                hf = a_f[up] * hf + b_f[up]
                pf = a_f[up] * pf
                hb = a_b[down] * hb + b_b[down]
                pb = a_b[down] * pb
                hfs.append(hf)
                pfs.append(pf)
                hbs.insert(0, hb)
                pbs.insert(0, pb)
            hf_ref[dst_f, :] = jnp.concatenate(hfs, axis=0)
            pf_ref[dst_f, :] = jnp.concatenate(pfs, axis=0)
            hb_ref[dst_b, :] = jnp.concatenate(hbs, axis=0)
            pb_ref[dst_b, :] = jnp.concatenate(pbs, axis=0)
            return hf, pf, hb, pb

        return lax.fori_loop(0, tch // SUBLANES, steps, carry)

    zeros = jnp.zeros((SUBLANES, LRU_BS), F32)
    ones = jnp.ones((SUBLANES, LRU_BS), F32)
    end_f, prod_f, end_b, prod_b = lax.fori_loop(0, n_chunk, chunk_pair, (zeros, ones, zeros, ones))

    cin_f = [jnp.zeros((1, LRU_BS), F32)]
    for s in range(SUBLANES - 1):
        cin_f.append(end_f[s:s + 1, :] + prod_f[s:s + 1, :] * cin_f[s])
    cin_b = [jnp.zeros((1, LRU_BS), F32)]
    for s in range(SUBLANES - 1, 0, -1):
        cin_b.insert(0, end_b[s:s + 1, :] + prod_b[s:s + 1, :] * cin_b[0])
    cin_f_tiles = pltpu.repeat(_rows_to_tile(cin_f, reverse=False), tch, axis=0)
    cin_b_tiles = pltpu.repeat(_rows_to_tile(cin_b, reverse=False), tch, axis=0)

    def fix_up(c, carry):
        rows = pl.ds(pl.multiple_of(c * chunk_rows, chunk_rows), chunk_rows)
        total = (hf_ref[rows, :] + pf_ref[rows, :] * cin_f_tiles) + (hb_ref[rows, :] + pb_ref[rows, :] * cin_b_tiles)
        for slab in range(n_slab):
            xi_ref[slab, rows, :] = total[:, slab * LANES:(slab + 1) * LANES]
        return carry

    lax.fori_loop(0, n_chunk, fix_up, 0)

    out_rows = 2 * SUBLANES

    def finish(tb, carry):
        for s in range(SUBLANES):
            halves = []
            for half in range(2):
                t0 = tb * out_rows + half * SUBLANES
                src = pl.ds(t0 * SUBLANES + s, SUBLANES, stride=SUBLANES)
                halves.append(jnp.concatenate([xi_ref[slab, src, :] for slab in range(n_slab)], axis=1))
            rows = pl.ds(pl.multiple_of(s * seg_len + tb * out_rows, out_rows), out_rows)
            y = jnp.concatenate(halves, axis=0) * jax.nn.gelu(gate_ref[rows, :])
            o_ref[rows, :] = y.astype(o_ref.dtype)
        return carry

    lax.fori_loop(0, seg_len // out_rows, finish, 0)


def _rglru(proj, conv_w, conv_b, w_fwd, w_bwd, bias, lam, tch):
    bsz, seq, width2 = proj.shape
    n_blk = width2 // (2 * LRU_BS)
    n_slab = LRU_BS // LANES
    seq_blk = lambda off: pl.BlockSpec((None, seq, LRU_BS), lambda b, n: (b, 0, off + n))
    per_blk = lambda rows, cols: pl.BlockSpec((None, rows, cols), lambda b, n: (n, 0, 0))
    slab_rows = pltpu.VMEM((seq, LRU_BS), F32)
    gate_rows = pltpu.VMEM((tch * SUBLANES, LRU_BS), F32)
    return pl.pallas_call(
        functools.partial(_rglru_kernel, seq=seq, tch=tch),
        grid=(bsz, n_blk),
        in_specs=[
            seq_blk(0),
            seq_blk(n_blk),
            pl.BlockSpec((CONV_WIDTH, LRU_BS), lambda b, n: (0, n)),
            pl.BlockSpec((1, LRU_BS), lambda b, n: (0, n)),
            per_blk(LRU_BS, 2 * LRU_BS),
            per_blk(LRU_BS, 2 * LRU_BS),
            per_blk(4, LRU_BS),
            per_blk(2, LRU_BS),
        ],
        out_specs=pl.BlockSpec((None, seq, LRU_BS), lambda b, n: (b, 0, n)),
        out_shape=jax.ShapeDtypeStruct((bsz, seq, n_blk * LRU_BS), BF16),
        scratch_shapes=[
            pltpu.VMEM((n_slab, seq + (CONV_WIDTH - 1) * SUBLANES, LANES), F32),
            pltpu.VMEM((seq, LRU_BS), F32),
            slab_rows, slab_rows, slab_rows, slab_rows,
            gate_rows, gate_rows, gate_rows, gate_rows,
        ],
        compiler_params=_params("arbitrary", "arbitrary"),
        name="rglru",
    )(proj, proj, conv_w, conv_b.reshape(1, -1), w_fwd, w_bwd, bias, lam)


def _even_layer(x, p, bsz, seq):
    m, d = x.shape
    s5_width = p["s5_w_glu"].shape[0]
    attn_width = N_KV_HEADS * Q_PER_KV * HEAD_DIM
    kv_width = N_KV_HEADS * HEAD_DIM
    proj = _norm_matmul(x, p["norm_mix"], p["w_in"], tm=512)

    ut = _s5_to_groups(proj, s5_width, chunks_per_step=256)
    yt = _s5_scan(ut, p["s5_w"], p["s5_m"], p["s5_v"], p["s5_a"], n_seq=bsz)
    y = _s5_from_groups(yt, chunks_per_step=256)
    y_s5 = _glu(y, p["s5_w_glu"], p["s5_b_glu"], tm=512)

    y_attn = _attention(proj.reshape(bsz, seq, -1), p["attn_sink"], p["attn_q_norm"], p["attn_k_norm"],
                        p["rope_cos"], p["rope_sin_lo"], p["rope_sin_hi"],
                        q_col0=s5_width, k_col0=s5_width + attn_width,
                        v_col0=s5_width + attn_width + kv_width, blocks_per_step=8)
    return [y_s5, y_attn.reshape(m, attn_width)], [p["w_out"][:s5_width], p["w_out"][s5_width:]]


def _odd_layer(x, p, bsz, seq):
    m, d = x.shape
    proj = _norm_matmul(x, p["norm_mix"], p["w_in"], tm=512)
    y = _rglru(proj.reshape(bsz, seq, -1), p["conv_w"], p["conv_b"], p["w_fwd"], p["w_bwd"],
               p["gate_bias"], p["lam"], tch=64)
    return [y.reshape(m, -1)], [p["w_out"]]


def kernel(x_prompt, x_sample, norm_mix, norm_ffn, ev_w_in, ev_w_out, s5_lam_re, s5_lam_im, s5_log_dt, s5_b_re, s5_b_im, s5_c_re, s5_c_im, s5_d, s5_w_glu, s5_b_glu, attn_q_norm, attn_k_norm, attn_sink, od_w_in, od_w_out, lru_conv_w, lru_conv_b, lru_wa, lru_ba, lru_wx, lru_bx, lru_lam, ffn_w1, ffn_w3, ffn_w2):
    depth = norm_mix.shape[0]
    seq_lens = {x_prompt.shape[1], x_sample.shape[1]}
    rope = {s: _rope_tables(s) for s in seq_lens}

    layers = []
    for layer in range(depth):
        if layer % 2 == 0:
            e = layer // 2
            s5_w, s5_m, s5_v, s5_a = _s5_operators(s5_lam_re[e], s5_lam_im[e], s5_log_dt[e], s5_b_re[e],
                                                   s5_b_im[e], s5_c_re[e], s5_c_im[e], s5_d[e])
            mix = dict(norm_mix=norm_mix[layer], w_in=ev_w_in[e].astype(BF16), w_out=ev_w_out[e].astype(BF16),
                       s5_w=s5_w, s5_m=s5_m, s5_v=s5_v, s5_a=s5_a,
                       s5_w_glu=s5_w_glu[e].astype(BF16), s5_b_glu=s5_b_glu[e],
                       attn_q_norm=attn_q_norm[e], attn_k_norm=attn_k_norm[e], attn_sink=attn_sink[e])
        else:
            o = layer // 2
            n_blk = lru_wa.shape[2]
            mix = dict(norm_mix=norm_mix[layer], w_in=od_w_in[o].astype(BF16), w_out=od_w_out[o].astype(BF16),
                       conv_w=0.5 * lru_conv_w[o], conv_b=0.5 * lru_conv_b[o],
                       w_fwd=jnp.concatenate([lru_wa[o, 0], lru_wx[o, 0]], axis=-1).astype(BF16),
                       w_bwd=jnp.concatenate([lru_wa[o, 1], lru_wx[o, 1]], axis=-1).astype(BF16),
                       gate_bias=(0.5 * jnp.stack([lru_ba[o, 0], lru_bx[o, 0], lru_ba[o, 1], lru_bx[o, 1]], axis=0))
                       .reshape(4, n_blk, LRU_BS).transpose(1, 0, 2),
                       lam=lru_lam[o].reshape(2, n_blk, LRU_BS).transpose(1, 0, 2))
        layers.append(mix)
    w1_all, w3_all, w2_all = ffn_w1.astype(BF16), ffn_w3.astype(BF16), ffn_w2.astype(BF16)

    def run(x):
        bsz, seq, d = x.shape
        h = x.reshape(bsz * seq, d)
        for layer, mix in enumerate(layers):
            if layer % 2 == 0:
                cos, sin_lo, sin_hi = rope[seq]
                acts, wos = _even_layer(h, dict(mix, rope_cos=cos, rope_sin_lo=sin_lo, rope_sin_hi=sin_hi), bsz, seq)
            else:
                acts, wos = _odd_layer(h, mix, bsz, seq)
            h = _mix_ffn(h, acts, wos, norm_ffn[layer], w1_all, w3_all, w2_all, layer, tm=512, tf=512)
        return h.reshape(bsz, seq, d)

    return (run(x_prompt), run(x_sample))
```

```python
import functools
import math

import jax
import jax.numpy as jnp
import numpy as np
from jax import lax
from jax.experimental import pallas as pl
from jax.experimental.pallas import tpu as pltpu

F32 = jnp.float32
BF16 = jnp.bfloat16

LANES = 128
SUBLANES = 8
VMEM_LIMIT_BYTES = 56 * 1024 * 1024

EPS = 1e-6
NEG_INF = -1e30
S5_GROUP = 16
S5_STATE = 64
HEAD_DIM = 128
N_KV_HEADS = 2
Q_PER_KV = 4
WINDOW = 128
BLOCK = 128
ROPE_HALF = 16
ROPE_THETA = 500000.0
LRU_BS = 256
LRU_C = 8.0
CONV_WIDTH = 4
CONV_LEFT = 2

S5_CHUNK = 16
S5_VEC = S5_CHUNK * S5_GROUP
S5_HALF = 2 * S5_STATE


def _params(*semantics):
    return pltpu.CompilerParams(dimension_semantics=semantics, vmem_limit_bytes=VMEM_LIMIT_BYTES)


def _rows_to_tile(rows, reverse):
    n = rows[0].shape[1]
    sub = lax.broadcasted_iota(jnp.int32, (SUBLANES, n), 0)
    tile = jnp.broadcast_to(rows[0], (SUBLANES, n))
    for j in range(1, SUBLANES):
        tile = jnp.where(sub == (SUBLANES - 1 - j if reverse else j), jnp.broadcast_to(rows[j], (SUBLANES, n)), tile)
    return tile


def _rms_normalize(x, gain):
    ms = jnp.mean(x * x, axis=-1, keepdims=True)
    return x * lax.rsqrt(ms + EPS) * gain


def _norm_matmul_kernel(x_ref, g_ref, w_ref, o_ref, hn_even_ref, hn_odd_ref):
    s = pl.program_id(0)

    @pl.when(s == 0)
    def _():
        hn_odd_ref[...] = jnp.zeros(hn_odd_ref.shape, BF16)

    def step(hn_prev_ref, hn_next_ref):
        o_ref[...] = jnp.dot(hn_prev_ref[...], w_ref[...], preferred_element_type=F32)
        hn_next_ref[...] = _rms_normalize(x_ref[...], g_ref[...]).astype(BF16)

    pl.when(s % 2 == 0)(functools.partial(step, hn_odd_ref, hn_even_ref))
    pl.when(s % 2 == 1)(functools.partial(step, hn_even_ref, hn_odd_ref))


def _norm_matmul(x, gain, w, tm):
    m, d = x.shape
    n = w.shape[1]
    n_blk = m // tm
    return pl.pallas_call(
        _norm_matmul_kernel,
        grid=(n_blk + 1,),
        in_specs=[
            pl.BlockSpec((tm, d), lambda s: (jnp.minimum(s, n_blk - 1), 0)),
            pl.BlockSpec((1, d), lambda s: (0, 0)),
            pl.BlockSpec((d, n), lambda s: (0, 0), pipeline_mode=pl.Buffered(1)),
        ],
        out_specs=pl.BlockSpec((tm, n), lambda s: (jnp.maximum(s - 1, 0), 0)),
        out_shape=jax.ShapeDtypeStruct((m, n), F32),
        scratch_shapes=[pltpu.VMEM((tm, d), BF16), pltpu.VMEM((tm, d), BF16)],
        compiler_params=_params("arbitrary"),
        name="norm_matmul",
    )(x, gain.reshape(1, d), w)


def _mix_ffn_kernel(*refs, n_in):
    x_ref = refs[0]
    a_refs = refs[1:1 + n_in]
    wo_refs = refs[1 + n_in:1 + 2 * n_in]
    g_ref, w1_ref, w3_ref, w2_ref, o_ref, hn_ref = refs[1 + 2 * n_in:]
    j = pl.program_id(1)

    @pl.when(j == 0)
    def _():
        x = x_ref[...]
        for a_ref, wo_ref in zip(a_refs, wo_refs):
            x = x + jnp.dot(a_ref[...], wo_ref[...], preferred_element_type=F32)
        hn_ref[...] = _rms_normalize(x, g_ref[...]).astype(BF16)
        o_ref[...] = x

    hn = hn_ref[...]
    a = jnp.dot(hn, w1_ref[...], preferred_element_type=F32)
    b = jnp.dot(hn, w3_ref[...], preferred_element_type=F32)
    act = (a * jax.nn.sigmoid(a) * b).astype(BF16)
    o_ref[...] += jnp.dot(act, w2_ref[...], preferred_element_type=F32)


def _mix_ffn(x, acts, wos, gain, w1, w3, w2, layer, tm, tf):
    m, d = x.shape
    f = w1.shape[2]
    n_in = len(acts)
    in_specs = [pl.BlockSpec((tm, d), lambda i, j: (i, 0))]
    in_specs += [pl.BlockSpec((tm, a.shape[1]), lambda i, j: (i, 0)) for a in acts]
    in_specs += [pl.BlockSpec(wo.shape, lambda i, j: (0, 0), pipeline_mode=pl.Buffered(1)) for wo in wos]
    in_specs += [
        pl.BlockSpec((1, d), lambda i, j: (0, 0)),
        pl.BlockSpec((None, d, tf), lambda i, j: (layer, 0, j)),
        pl.BlockSpec((None, d, tf), lambda i, j: (layer, 0, j)),
        pl.BlockSpec((None, tf, d), lambda i, j: (layer, j, 0)),
    ]
    return pl.pallas_call(
        functools.partial(_mix_ffn_kernel, n_in=n_in),
        grid=(m // tm, f // tf),
        in_specs=in_specs,
        out_specs=pl.BlockSpec((tm, d), lambda i, j: (i, 0)),
        out_shape=jax.ShapeDtypeStruct((m, d), F32),
        scratch_shapes=[pltpu.VMEM((tm, d), BF16)],
        compiler_params=_params("arbitrary", "arbitrary"),
        name="mix_ffn",
    )(x, *acts, *wos, gain.reshape(1, d), w1, w3, w2)


def _glu_kernel(y_ref, w_ref, b_ref, o_ref):
    g = jax.nn.gelu(y_ref[...])
    z = jnp.dot(g.astype(BF16), w_ref[...], preferred_element_type=F32) + b_ref[...]
    o_ref[...] = (g * jax.nn.sigmoid(z)).astype(o_ref.dtype)


def _glu(y, w, b, tm):
    m, c = y.shape
    return pl.pallas_call(
        _glu_kernel,
        grid=(m // tm,),
        in_specs=[
            pl.BlockSpec((tm, c), lambda i: (i, 0)),
            pl.BlockSpec((c, c), lambda i: (0, 0)),
            pl.BlockSpec((1, c), lambda i: (0, 0)),
        ],
        out_specs=pl.BlockSpec((tm, c), lambda i: (i, 0)),
        out_shape=jax.ShapeDtypeStruct((m, c), BF16),
        compiler_params=_params("arbitrary"),
        name="s5_glu",
    )(y, w, b.reshape(1, c))


def _s5_kernel(ut_ref, w_ref, m_ref, v_ref, a_ref, yt_ref, s_ref, hf_ref, hb_ref, *, n_seq, n_chunk):
    u = ut_ref[...].T.astype(BF16)
    s = jnp.dot(u, w_ref[...], preferred_element_type=F32)
    s_ref[0] = s[:, :S5_HALF]
    s_ref[1] = s[:, S5_HALF:]
    ar = a_ref[0:1, :]
    ai = a_ref[1:2, :]
    is_fwd = lax.broadcasted_iota(jnp.int32, (1, S5_HALF), 1) < S5_STATE

    def step(i, carry):
        hre, him = carry
        rows_f = pl.ds(i, n_seq, stride=n_chunk)
        rows_b = pl.ds(n_chunk - 1 - i, n_seq, stride=n_chunk)
        hf_ref[0, rows_f, :] = hre
        hf_ref[1, rows_f, :] = him
        hb_ref[0, rows_b, :] = hre
        hb_ref[1, rows_b, :] = him
        sre = jnp.where(is_fwd, s_ref[0, rows_f, :], s_ref[0, rows_b, :])
        sim = jnp.where(is_fwd, s_ref[1, rows_f, :], s_ref[1, rows_b, :])
        return ar * hre - ai * him + sre, ar * him + ai * hre + sim

    zero = jnp.zeros((n_seq, S5_HALF), F32)
    lax.fori_loop(0, n_chunk, step, (zero, zero), unroll=8)

    h_re = jnp.where(is_fwd, hf_ref[0], hb_ref[0])
    h_im = jnp.where(is_fwd, hf_ref[1], hb_ref[1])
    h = jnp.concatenate([h_re, h_im], axis=1).astype(BF16)
    y = (jnp.dot(u, m_ref[...], preferred_element_type=F32)
         + jnp.dot(h, v_ref[...], preferred_element_type=F32))
    yt_ref[...] = y.T


def _s5_scan(ut, w, m, v, a_t, n_seq):
    groups, vec, rows = ut.shape
    n_chunk = rows // n_seq
    return pl.pallas_call(
        functools.partial(_s5_kernel, n_seq=n_seq, n_chunk=n_chunk),
        grid=(groups,),
        in_specs=[
            pl.BlockSpec((None, vec, rows), lambda g: (g, 0, 0)),
            pl.BlockSpec((None, vec, 2 * S5_HALF), lambda g: (g, 0, 0)),
            pl.BlockSpec((None, vec, vec), lambda g: (g, 0, 0)),
            pl.BlockSpec((None, 2 * S5_HALF, vec), lambda g: (g, 0, 0)),
            pl.BlockSpec((None, 2, S5_HALF), lambda g: (g, 0, 0)),
        ],
        out_specs=pl.BlockSpec((None, vec, rows), lambda g: (g, 0, 0)),
        out_shape=jax.ShapeDtypeStruct((groups, vec, rows), F32),
        scratch_shapes=[
            pltpu.VMEM((2, rows, S5_HALF), F32),
            pltpu.VMEM((2, rows, S5_HALF), F32),
            pltpu.VMEM((2, rows, S5_HALF), F32),
        ],
        compiler_params=_params("arbitrary"),
        name="s5_scan",
    )(ut, w, m, v, a_t)


GROUPS_PER_LANE_BLOCK = LANES // S5_GROUP


def _to_groups_kernel(x_ref, o_ref):
    n_chunk = o_ref.shape[2]
    for tau in range(S5_CHUNK):
        rows = x_ref[pl.ds(tau, n_chunk, stride=S5_CHUNK), :]
        o_ref[:, tau * S5_GROUP:(tau + 1) * S5_GROUP, :] = rows.T.reshape(GROUPS_PER_LANE_BLOCK, S5_GROUP, n_chunk)


def _from_groups_kernel(y_ref, o_ref):
    n_chunk = y_ref.shape[2]
    for tau in range(S5_CHUNK):
        slab = y_ref[:, tau * S5_GROUP:(tau + 1) * S5_GROUP, :].reshape(LANES, n_chunk)
        o_ref[pl.ds(tau, n_chunk, stride=S5_CHUNK), :] = slab.T


def _s5_to_groups(x, width, chunks_per_step):
    tokens = x.shape[0]
    n_chunk = tokens // S5_CHUNK
    return pl.pallas_call(
        _to_groups_kernel,
        grid=(n_chunk // chunks_per_step, width // LANES),
        in_specs=[pl.BlockSpec((chunks_per_step * S5_CHUNK, LANES), lambda i, c: (i, c))],
        out_specs=pl.BlockSpec((GROUPS_PER_LANE_BLOCK, S5_VEC, chunks_per_step), lambda i, c: (c, 0, i)),
        out_shape=jax.ShapeDtypeStruct((width // S5_GROUP, S5_VEC, n_chunk), F32),
        compiler_params=_params("arbitrary", "arbitrary"),
        name="s5_to_groups",
    )(x)


def _s5_from_groups(yt, chunks_per_step):
    groups, _, n_chunk = yt.shape
    width = groups * S5_GROUP
    return pl.pallas_call(
        _from_groups_kernel,
        grid=(n_chunk // chunks_per_step, width // LANES),
        in_specs=[pl.BlockSpec((GROUPS_PER_LANE_BLOCK, S5_VEC, chunks_per_step), lambda i, c: (c, 0, i))],
        out_specs=pl.BlockSpec((chunks_per_step * S5_CHUNK, LANES), lambda i, c: (i, c)),
        out_shape=jax.ShapeDtypeStruct((n_chunk * S5_CHUNK, width), F32),
        compiler_params=_params("arbitrary", "arbitrary"),
        name="s5_from_groups",
    )(yt)


def _s5_operators(lam_re, lam_im, log_dt, b_re, b_im, c_re, c_im, d_skip):
    hi = lax.Precision.HIGHEST
    t = S5_CHUNK
    groups = lam_re.shape[1]
    lr = jnp.minimum(lam_re.astype(F32), -1e-4)
    li = lam_im.astype(F32)
    dt = jnp.exp(log_dt.astype(F32))[..., None]
    mag = jnp.exp(lr * dt)
    ab_re = mag * jnp.cos(li * dt)
    ab_im = mag * jnp.sin(li * dt)
    nr, ni = ab_re - 1.0, ab_im
    den = lr * lr + li * li
    f_re = (nr * lr + ni * li) / den
    f_im = (ni * lr - nr * li) / den
    br, bi = b_re.astype(F32), b_im.astype(F32)
    bb_re = f_re[..., None] * br - f_im[..., None] * bi
    bb_im = f_re[..., None] * bi + f_im[..., None] * br
    cr, ci = c_re.astype(F32), c_im.astype(F32)

    k = jnp.arange(t + 1, dtype=F32)
    mag_k = jnp.exp((lr * dt)[..., None] * k)
    ang_k = (li * dt)[..., None] * k
    pr = mag_k * jnp.cos(ang_k)
    pi = mag_k * jnp.sin(ang_k)

    ab_r = pr[..., :t, None] * bb_re[..., None, :] - pi[..., :t, None] * bb_im[..., None, :]
    ab_i = pr[..., :t, None] * bb_im[..., None, :] + pi[..., :t, None] * bb_re[..., None, :]
    lag = (jnp.einsum('zgnp,zgpkm->zgknm', cr, ab_r, precision=hi)
           - jnp.einsum('zgnp,zgpkm->zgknm', ci, ab_i, precision=hi))

    k_idx = np.arange(t)[:, None, None]
    s_idx = np.arange(t)[None, :, None]
    t_idx = np.arange(t)[None, None, :]
    sel_f = jnp.asarray((t_idx - s_idx == k_idx).astype(np.float32))
    sel_b = jnp.asarray((s_idx - t_idx == k_idx).astype(np.float32))
    skip = jnp.einsum('st,gn,nm->gsmtn', jnp.eye(t, dtype=F32), d_skip.astype(F32).reshape(groups, S5_GROUP),
                      jnp.eye(S5_GROUP, dtype=F32), precision=hi)
    m_op = (jnp.einsum('kst,gknm->gsmtn', sel_f, lag[0], precision=hi)
            + jnp.einsum('kst,gknm->gsmtn', sel_b, lag[1], precision=hi)
            + skip).reshape(groups, S5_VEC, S5_VEC)

    pf_r, pf_i = pr[0][..., :t][..., ::-1], pi[0][..., :t][..., ::-1]
    pb_r, pb_i = pr[1][..., :t], pi[1][..., :t]

    def contrib(p_r, p_i, z):
        re = p_r[..., None] * bb_re[z][:, :, None, :] - p_i[..., None] * bb_im[z][:, :, None, :]
        im = p_r[..., None] * bb_im[z][:, :, None, :] + p_i[..., None] * bb_re[z][:, :, None, :]
        to_rows = lambda x: x.transpose(0, 2, 3, 1).reshape(groups, S5_VEC, S5_STATE)
        return to_rows(re), to_rows(im)

    wf_re, wf_im = contrib(pf_r, pf_i, 0)
    wb_re, wb_im = contrib(pb_r, pb_i, 1)
    w_op = jnp.concatenate([wf_re, wb_re, wf_im, wb_im], axis=-1)

    qf_r, qf_i = pr[0][..., 1:], pi[0][..., 1:]
    qb_r, qb_i = pr[1][..., 1:][..., ::-1], pi[1][..., 1:][..., ::-1]

    def readout(q_r, q_i, z):
        c_r = cr[z].transpose(0, 2, 1)[:, :, None, :]
        c_i = ci[z].transpose(0, 2, 1)[:, :, None, :]
        w_r = c_r * q_r[..., None] - c_i * q_i[..., None]
        w_i = c_r * q_i[..., None] + c_i * q_r[..., None]
        return w_r.reshape(groups, S5_STATE, S5_VEC), (-w_i).reshape(groups, S5_STATE, S5_VEC)

    vf_re, vf_im = readout(qf_r, qf_i, 0)
    vb_re, vb_im = readout(qb_r, qb_i, 1)
    v_op = jnp.concatenate([vf_re, vb_re, vf_im, vb_im], axis=1)

    a_t = jnp.stack([jnp.concatenate([pr[0][..., t], pr[1][..., t]], axis=-1),
                     jnp.concatenate([pi[0][..., t], pi[1][..., t]], axis=-1)], axis=1)
    return w_op.astype(BF16), m_op.astype(BF16), v_op.astype(BF16), a_t


def _rope(x, cos, sin_lo, sin_hi):
    return (x * cos
            + pltpu.roll(x, HEAD_DIM - ROPE_HALF, 1) * sin_lo
            + pltpu.roll(x, ROPE_HALF, 1) * sin_hi)


def _attn_kernel(sink_ref, q_ref, k_ref, v_ref, qn_ref, kn_ref, cos_ref, slo_ref, shi_ref,
                 o_ref, kb_ref, vb_ref, *, seq, blocks_per_step):
    h = pl.program_id(1)
    n = pl.program_id(2)
    prep_rows = 256

    @pl.when(n == 0)
    def _():
        pad = jnp.zeros((BLOCK, HEAD_DIM), BF16)
        kb_ref[0:BLOCK, :] = pad
        vb_ref[0:BLOCK, :] = pad
        kb_ref[BLOCK + seq:, :] = pad
        vb_ref[BLOCK + seq:, :] = pad

        def prep(c, carry):
            r0 = pl.multiple_of(c * prep_rows, prep_rows)
            rows = pl.ds(r0, prep_rows)
            k = _rms_normalize(k_ref[rows, :], kn_ref[...])
            k = _rope(k, cos_ref[rows, :], slo_ref[rows, :], shi_ref[rows, :])
            kb_ref[pl.ds(r0 + BLOCK, prep_rows), :] = k.astype(BF16)
            vb_ref[pl.ds(r0 + BLOCK, prep_rows), :] = v_ref[rows, :].astype(BF16)
            return carry

        lax.fori_loop(0, seq // prep_rows, prep, 0)

    qi = lax.broadcasted_iota(jnp.int32, (BLOCK, 1), 0)
    kj = lax.broadcasted_iota(jnp.int32, (BLOCK, 3 * BLOCK), 1)

    def query_block(qb, carry):
        local = pl.multiple_of(qb * BLOCK, BLOCK)
        r0 = pl.multiple_of((n * blocks_per_step + qb) * BLOCK, BLOCK)
        rows = pl.ds(r0, BLOCK)
        cos, slo, shi = cos_ref[rows, :], slo_ref[rows, :], shi_ref[rows, :]
        heads = []
        for g in range(Q_PER_KV):
            qg = _rms_normalize(q_ref[pl.ds(local, BLOCK), g * HEAD_DIM:(g + 1) * HEAD_DIM], qn_ref[...])
            heads.append(_rope(qg, cos, slo, shi).astype(BF16))
        q4 = jnp.concatenate(heads, axis=0)
        kw = kb_ref[pl.ds(r0, 3 * BLOCK), :]
        vw = vb_ref[pl.ds(r0, 3 * BLOCK), :]
        s = lax.dot_general(q4, kw, (((1,), (1,)), ((), ())), preferred_element_type=F32)
        s = s * (HEAD_DIM ** -0.5)

        lo = jnp.maximum(qi + (BLOCK - WINDOW), BLOCK - r0)
        hi = jnp.minimum(qi + (BLOCK + WINDOW), seq - 1 + BLOCK - r0)
        probs = []
        inv_denoms = []
        for g in range(Q_PER_KV):
            sg = s[g * BLOCK:(g + 1) * BLOCK, :]
            sg = jnp.where(kj >= lo, jnp.where(kj <= hi, sg, NEG_INF), NEG_INF)
            sk = sink_ref[h * Q_PER_KV + g]
            mx = jnp.maximum(jnp.max(sg, axis=-1, keepdims=True), sk)
            p = jnp.exp(sg - mx)
            denom = jnp.sum(p, axis=-1, keepdims=True) + jnp.exp(sk - mx)
            probs.append(p.astype(BF16))
            inv_denoms.append(1.0 / denom)
        p4 = jnp.concatenate(probs, axis=0)
        o4 = jnp.dot(p4, vw, preferred_element_type=F32)
        for g in range(Q_PER_KV):
            og = o4[g * BLOCK:(g + 1) * BLOCK, :] * inv_denoms[g]
            o_ref[pl.ds(local, BLOCK), g * HEAD_DIM:(g + 1) * HEAD_DIM] = og.astype(o_ref.dtype)
        return carry

    lax.fori_loop(0, blocks_per_step, query_block, 0)


def _attention(proj, sink, q_norm, k_norm, cos, sin_lo, sin_hi, q_col0, k_col0, v_col0, blocks_per_step):
    bsz, seq, _ = proj.shape
    q_w = Q_PER_KV * HEAD_DIM
    q_blk0, k_blk0, v_blk0 = q_col0 // q_w, k_col0 // HEAD_DIM, v_col0 // HEAD_DIM
    table = pl.BlockSpec((seq, HEAD_DIM), lambda b, h, n: (0, 0))
    step_rows = blocks_per_step * BLOCK
    return pl.pallas_call(
        functools.partial(_attn_kernel, seq=seq, blocks_per_step=blocks_per_step),
        grid=(bsz, N_KV_HEADS, seq // step_rows),
        in_specs=[
            pl.BlockSpec(memory_space=pltpu.SMEM),
            pl.BlockSpec((None, step_rows, q_w), lambda b, h, n: (b, n, q_blk0 + h)),
            pl.BlockSpec((None, seq, HEAD_DIM), lambda b, h, n: (b, 0, k_blk0 + h)),
            pl.BlockSpec((None, seq, HEAD_DIM), lambda b, h, n: (b, 0, v_blk0 + h)),
            pl.BlockSpec((1, HEAD_DIM), lambda b, h, n: (0, 0)),
            pl.BlockSpec((1, HEAD_DIM), lambda b, h, n: (0, 0)),
            table, table, table,
        ],
        out_specs=pl.BlockSpec((None, step_rows, q_w), lambda b, h, n: (b, n, h)),
        out_shape=jax.ShapeDtypeStruct((bsz, seq, N_KV_HEADS * q_w), BF16),
        scratch_shapes=[
            pltpu.VMEM((seq + 2 * BLOCK, HEAD_DIM), BF16),
            pltpu.VMEM((seq + 2 * BLOCK, HEAD_DIM), BF16),
        ],
        compiler_params=_params("arbitrary", "arbitrary", "arbitrary"),
        name="window_attention",
    )(sink, proj, proj, proj, q_norm.reshape(1, HEAD_DIM), k_norm.reshape(1, HEAD_DIM), cos, sin_lo, sin_hi)


def _rope_tables(seq):
    inv = ROPE_THETA ** (-jnp.arange(ROPE_HALF, dtype=F32) / ROPE_HALF)
    ang = jnp.arange(seq).astype(F32)[:, None] * inv[None, :]
    cos, sin = jnp.cos(ang), jnp.sin(ang)
    rest = HEAD_DIM - 2 * ROPE_HALF
    cos_t = jnp.concatenate([cos, cos, jnp.ones((seq, rest), F32)], axis=-1)
    sin_lo = jnp.concatenate([-sin, jnp.zeros((seq, HEAD_DIM - ROPE_HALF), F32)], axis=-1)
    sin_hi = jnp.concatenate([jnp.zeros((seq, ROPE_HALF), F32), sin, jnp.zeros((seq, rest), F32)], axis=-1)
    return cos_t, sin_lo, sin_hi


def _rglru_kernel(gate_ref, xb_ref, cw_ref, cb_ref, wf_ref, wb_ref, bias_ref, lam_ref, o_ref,
                  xi_ref, conv_ref, hf_ref, pf_ref, hb_ref, pb_ref, af_ref, bf_ref, ab_ref, bb_ref, *, seq, tch):
    seg_len = seq // SUBLANES
    n_slab = LRU_BS // LANES
    halo = CONV_LEFT * SUBLANES
    n_chunk = seg_len // tch
    chunk_rows = tch * SUBLANES

    def interleave(tb, carry):
        t0 = tb * SUBLANES
        for s in range(SUBLANES):
            tile = xb_ref[pl.ds(pl.multiple_of(s * seg_len + t0, SUBLANES), SUBLANES), :]
            dst = pl.ds(halo + t0 * SUBLANES + s, SUBLANES, stride=SUBLANES)
            for slab in range(n_slab):
                xi_ref[slab, dst, :] = tile[:, slab * LANES:(slab + 1) * LANES]
        return carry

    lax.fori_loop(0, seg_len // SUBLANES, interleave, 0)

    sub = lax.broadcasted_iota(jnp.int32, (SUBLANES, LANES), 0)
    for slab in range(n_slab):
        for back in range(1, CONV_LEFT + 1):
            src = halo + (seg_len - back) * SUBLANES
            tile = pltpu.roll(xi_ref[slab, src:src + SUBLANES, :], 1, 0)
            xi_ref[slab, halo - back * SUBLANES:halo - (back - 1) * SUBLANES, :] = jnp.where(sub == 0, 0.0, tile)
        tile = pltpu.roll(xi_ref[slab, halo:halo + SUBLANES, :], SUBLANES - 1, 0)
        end = halo + seq
        xi_ref[slab, end:end + SUBLANES, :] = jnp.where(sub == SUBLANES - 1, 0.0, tile)

    cw = cw_ref[...]
    cb = cb_ref[...]

    def conv_chunk(c, carry):
        t0 = c * tch
        parts = []
        for slab in range(n_slab):
            cols = slice(slab * LANES, (slab + 1) * LANES)
            acc = cb[:, cols]
            for tap in range(CONV_WIDTH):
                r0 = pl.multiple_of((t0 + tap) * SUBLANES, SUBLANES)
                acc = acc + cw[tap:tap + 1, cols] * xi_ref[slab, pl.ds(r0, chunk_rows), :]
            parts.append(acc)
        conv_ref[pl.ds(pl.multiple_of(t0 * SUBLANES, SUBLANES), chunk_rows), :] = jnp.concatenate(parts, axis=1)
        return carry

    lax.fori_loop(0, n_chunk, conv_chunk, 0)

    def softplus(x):
        return jnp.maximum(x, 0.0) + jnp.log1p(jnp.exp(-jnp.abs(x)))

    def gates(t0, w_ref, z, a_out, b_out):
        half_conv = conv_ref[pl.ds(pl.multiple_of(t0 * SUBLANES, SUBLANES), chunk_rows), :]
        pre = jnp.dot(half_conv.astype(BF16), w_ref[...], preferred_element_type=F32)
        t_r = jnp.tanh(pre[:, :LRU_BS] + bias_ref[2 * z:2 * z + 1, :])
        t_i = jnp.tanh(pre[:, LRU_BS:] + bias_ref[2 * z + 1:2 * z + 2, :])
        c1 = (-0.5 * LRU_C) * softplus(-lam_ref[z:z + 1, :])
        a = jnp.exp(c1 * t_r + c1)
        a_out[...] = a
        v = 1.0 - a * a
        root = jnp.where(v > 0.0, v * lax.rsqrt(v), 0.0)
        b_out[...] = root * (half_conv * t_i + half_conv)

    def chunk_pair(c, carry):
        tf0 = c * tch
        tb0 = (n_chunk - 1 - c) * tch
        gates(tf0, wf_ref, 0, af_ref, bf_ref)
        gates(tb0, wb_ref, 1, ab_ref, bb_ref)

        def steps(k, state):
            hf, pf, hb, pb = state
            steps_per_trip = SUBLANES
            group = steps_per_trip * SUBLANES
            src_f = pl.ds(pl.multiple_of(k * group, group), group)
            src_b = pl.ds(pl.multiple_of(chunk_rows - group - k * group, group), group)
            dst_f = pl.ds(pl.multiple_of(tf0 * SUBLANES + k * group, group), group)
            dst_b = pl.ds(pl.multiple_of(tb0 * SUBLANES + chunk_rows - group - k * group, group), group)
            a_f, b_f = af_ref[src_f, :], bf_ref[src_f, :]
            a_b, b_b = ab_ref[src_b, :], bb_ref[src_b, :]
            hfs, pfs, hbs, pbs = [], [], [], []
            for j in range(steps_per_trip):
                up = slice(j * SUBLANES, (j + 1) * SUBLANES)
                down = slice((steps_per_trip - 1 - j) * SUBLANES, (steps_per_trip - j) * SUBLANES)
                hf = a_f[up] * hf + b_f[up]
                pf = a_f[up] * pf
                hb = a_b[down] * hb + b_b[down]
                pb = a_b[down] * pb
                hfs.append(hf)
                pfs.append(pf)
                hbs.insert(0, hb)
                pbs.insert(0, pb)
            hf_ref[dst_f, :] = jnp.concatenate(hfs, axis=0)
            pf_ref[dst_f, :] = jnp.concatenate(pfs, axis=0)
            hb_ref[dst_b, :] = jnp.concatenate(hbs, axis=0)
            pb_ref[dst_b, :] = jnp.concatenate(pbs, axis=0)
            return hf, pf, hb, pb

        return lax.fori_loop(0, tch // SUBLANES, steps, carry)

    zeros = jnp.zeros((SUBLANES, LRU_BS), F32)
    ones = jnp.ones((SUBLANES, LRU_BS), F32)
    end_f, prod_f, end_b, prod_b = lax.fori_loop(0, n_chunk, chunk_pair, (zeros, ones, zeros, ones))

    cin_f = [jnp.zeros((1, LRU_BS), F32)]
    for s in range(SUBLANES - 1):
        cin_f.append(end_f[s:s + 1, :] + prod_f[s:s + 1, :] * cin_f[s])
    cin_b = [jnp.zeros((1, LRU_BS), F32)]
    for s in range(SUBLANES - 1, 0, -1):
        cin_b.insert(0, end_b[s:s + 1, :] + prod_b[s:s + 1, :] * cin_b[0])
    def per_chunk(rows):
        tile = _rows_to_tile(rows, reverse=False)
        return jnp.broadcast_to(tile[None], (tch, SUBLANES, LRU_BS)).reshape(chunk_rows, LRU_BS)

    cin_f_tiles = per_chunk(cin_f)
    cin_b_tiles = per_chunk(cin_b)

    def fix_up(c, carry):
        rows = pl.ds(pl.multiple_of(c * chunk_rows, chunk_rows), chunk_rows)
        total = (hf_ref[rows, :] + pf_ref[rows, :] * cin_f_tiles) + (hb_ref[rows, :] + pb_ref[rows, :] * cin_b_tiles)
        for slab in range(n_slab):
            xi_ref[slab, rows, :] = total[:, slab * LANES:(slab + 1) * LANES]
        return carry

    lax.fori_loop(0, n_chunk, fix_up, 0)

    out_rows = 2 * SUBLANES

    def finish(tb, carry):
        for s in range(SUBLANES):
            halves = []
            for half in range(2):
                t0 = tb * out_rows + half * SUBLANES
                src = pl.ds(t0 * SUBLANES + s, SUBLANES, stride=SUBLANES)
                halves.append(jnp.concatenate([xi_ref[slab, src, :] for slab in range(n_slab)], axis=1))
            rows = pl.ds(pl.multiple_of(s * seg_len + tb * out_rows, out_rows), out_rows)
            y = jnp.concatenate(halves, axis=0) * jax.nn.gelu(gate_ref[rows, :])
            o_ref[rows, :] = y.astype(o_ref.dtype)
        return carry

    lax.fori_loop(0, seg_len // out_rows, finish, 0)


def _rglru(proj, conv_w, conv_b, w_fwd, w_bwd, bias, lam, tch):
    bsz, seq, width2 = proj.shape
    n_blk = width2 // (2 * LRU_BS)
    n_slab = LRU_BS // LANES
    seq_blk = lambda off: pl.BlockSpec((None, seq, LRU_BS), lambda b, n: (b, 0, off + n))
    per_blk = lambda rows, cols: pl.BlockSpec((None, rows, cols), lambda b, n: (n, 0, 0))
    slab_rows = pltpu.VMEM((seq, LRU_BS), F32)
    gate_rows = pltpu.VMEM((tch * SUBLANES, LRU_BS), F32)
    return pl.pallas_call(
        functools.partial(_rglru_kernel, seq=seq, tch=tch),
        grid=(bsz, n_blk),
        in_specs=[
            seq_blk(0),
            seq_blk(n_blk),
            pl.BlockSpec((CONV_WIDTH, LRU_BS), lambda b, n: (0, n)),
            pl.BlockSpec((1, LRU_BS), lambda b, n: (0, n)),
            per_blk(LRU_BS, 2 * LRU_BS),
            per_blk(LRU_BS, 2 * LRU_BS),
            per_blk(4, LRU_BS),
            per_blk(2, LRU_BS),
        ],
        out_specs=pl.BlockSpec((None, seq, LRU_BS), lambda b, n: (b, 0, n)),
        out_shape=jax.ShapeDtypeStruct((bsz, seq, n_blk * LRU_BS), BF16),
        scratch_shapes=[
            pltpu.VMEM((n_slab, seq + (CONV_WIDTH - 1) * SUBLANES, LANES), F32),
            pltpu.VMEM((seq, LRU_BS), F32),
            slab_rows, slab_rows, slab_rows, slab_rows,
            gate_rows, gate_rows, gate_rows, gate_rows,
        ],
        compiler_params=_params("arbitrary", "arbitrary"),
        name="rglru",
    )(proj, proj, conv_w, conv_b.reshape(1, -1), w_fwd, w_bwd, bias, lam)


def _even_layer(x, p, bsz, seq):
    m, d = x.shape
    s5_width = p["s5_w_glu"].shape[0]
    attn_width = N_KV_HEADS * Q_PER_KV * HEAD_DIM
    kv_width = N_KV_HEADS * HEAD_DIM
    proj = _norm_matmul(x, p["norm_mix"], p["w_in"], tm=512)

    ut = _s5_to_groups(proj, s5_width, chunks_per_step=256)
    yt = _s5_scan(ut, p["s5_w"], p["s5_m"], p["s5_v"], p["s5_a"], n_seq=bsz)
    y = _s5_from_groups(yt, chunks_per_step=256)
    y_s5 = _glu(y, p["s5_w_glu"], p["s5_b_glu"], tm=512)

    y_attn = _attention(proj.reshape(bsz, seq, -1), p["attn_sink"], p["attn_q_norm"], p["attn_k_norm"],
                        p["rope_cos"], p["rope_sin_lo"], p["rope_sin_hi"],
                        q_col0=s5_width, k_col0=s5_width + attn_width,
                        v_col0=s5_width + attn_width + kv_width, blocks_per_step=8)
    return [y_s5, y_attn.reshape(m, attn_width)], [p["w_out"][:s5_width], p["w_out"][s5_width:]]


def _odd_layer(x, p, bsz, seq):
    m, d = x.shape
    proj = _norm_matmul(x, p["norm_mix"], p["w_in"], tm=512)
    y = _rglru(proj.reshape(bsz, seq, -1), p["conv_w"], p["conv_b"], p["w_fwd"], p["w_bwd"],
               p["gate_bias"], p["lam"], tch=64)
    return [y.reshape(m, -1)], [p["w_out"]]


def kernel(x_prompt, x_sample, norm_mix, norm_ffn, ev_w_in, ev_w_out, s5_lam_re, s5_lam_im, s5_log_dt, s5_b_re, s5_b_im, s5_c_re, s5_c_im, s5_d, s5_w_glu, s5_b_glu, attn_q_norm, attn_k_norm, attn_sink, od_w_in, od_w_out, lru_conv_w, lru_conv_b, lru_wa, lru_ba, lru_wx, lru_bx, lru_lam, ffn_w1, ffn_w3, ffn_w2):
    depth = norm_mix.shape[0]
    seq_lens = {x_prompt.shape[1], x_sample.shape[1]}
    rope = {s: _rope_tables(s) for s in seq_lens}

    layers = []
    for layer in range(depth):
        if layer % 2 == 0:
            e = layer // 2
            s5_w, s5_m, s5_v, s5_a = _s5_operators(s5_lam_re[e], s5_lam_im[e], s5_log_dt[e], s5_b_re[e],
                                                   s5_b_im[e], s5_c_re[e], s5_c_im[e], s5_d[e])
            mix = dict(norm_mix=norm_mix[layer], w_in=ev_w_in[e].astype(BF16), w_out=ev_w_out[e].astype(BF16),
                       s5_w=s5_w, s5_m=s5_m, s5_v=s5_v, s5_a=s5_a,
                       s5_w_glu=s5_w_glu[e].astype(BF16), s5_b_glu=s5_b_glu[e],
                       attn_q_norm=attn_q_norm[e], attn_k_norm=attn_k_norm[e], attn_sink=attn_sink[e])
        else:
            o = layer // 2
            n_blk = lru_wa.shape[2]
            mix = dict(norm_mix=norm_mix[layer], w_in=od_w_in[o].astype(BF16), w_out=od_w_out[o].astype(BF16),
                       conv_w=0.5 * lru_conv_w[o], conv_b=0.5 * lru_conv_b[o],
                       w_fwd=jnp.concatenate([lru_wa[o, 0], lru_wx[o, 0]], axis=-1).astype(BF16),
                       w_bwd=jnp.concatenate([lru_wa[o, 1], lru_wx[o, 1]], axis=-1).astype(BF16),
                       gate_bias=(0.5 * jnp.stack([lru_ba[o, 0], lru_bx[o, 0], lru_ba[o, 1], lru_bx[o, 1]], axis=0))
                       .reshape(4, n_blk, LRU_BS).transpose(1, 0, 2),
                       lam=lru_lam[o].reshape(2, n_blk, LRU_BS).transpose(1, 0, 2))
        layers.append(mix)
    w1_all, w3_all, w2_all = ffn_w1.astype(BF16), ffn_w3.astype(BF16), ffn_w2.astype(BF16)

    def run(x):
        bsz, seq, d = x.shape
        h = x.reshape(bsz * seq, d)
        for layer, mix in enumerate(layers):
            if layer % 2 == 0:
                cos, sin_lo, sin_hi = rope[seq]
                acts, wos = _even_layer(h, dict(mix, rope_cos=cos, rope_sin_lo=sin_lo, rope_sin_hi=sin_hi), bsz, seq)
            else:
                acts, wos = _odd_layer(h, mix, bsz, seq)
            h = _mix_ffn(h, acts, wos, norm_ffn[layer], w1_all, w3_all, w2_all, layer, tm=512, tf=512)
        return h.reshape(bsz, seq, d)

    return (run(x_prompt), run(x_sample))
```

```python
import functools

import jax
import jax.numpy as jnp
import numpy as np
from jax import lax
from jax.experimental import pallas as pl
from jax.experimental.pallas import tpu as pltpu

F32 = jnp.float32
BF16 = jnp.bfloat16

LANES = 128
SUBLANES = 8
VMEM_LIMIT_BYTES = 56 * 1024 * 1024

EPS = 1e-6
NEG_INF = -1e30
S5_GROUP = 16
S5_STATE = 64
HEAD_DIM = 128
N_KV_HEADS = 2
Q_PER_KV = 4
WINDOW = 128
BLOCK = 128
ROPE_HALF = 16
ROPE_THETA = 500000.0
LRU_BS = 256
LRU_C = 8.0
CONV_WIDTH = 4
CONV_LEFT = 2

ROW_TILE = 512
FF_TILE = 512
RELAYOUT_CHUNKS = 256
ATTN_BLOCKS_PER_STEP = 8
LRU_TIME_CHUNK = 64

S5_CHUNK = 16
S5_VEC = S5_CHUNK * S5_GROUP
S5_HALF = 2 * S5_STATE


def _params(*semantics):
    return pltpu.CompilerParams(dimension_semantics=semantics, vmem_limit_bytes=VMEM_LIMIT_BYTES)


def _rows_to_tile(rows, reverse):
    n = rows[0].shape[1]
    sub = lax.broadcasted_iota(jnp.int32, (SUBLANES, n), 0)
    tile = jnp.broadcast_to(rows[0], (SUBLANES, n))
    for j in range(1, SUBLANES):
        tile = jnp.where(sub == (SUBLANES - 1 - j if reverse else j), jnp.broadcast_to(rows[j], (SUBLANES, n)), tile)
    return tile


def _rms_normalize(x, gain):
    ms = jnp.mean(x * x, axis=-1, keepdims=True)
    return x * lax.rsqrt(ms + EPS) * gain


def _norm_matmul_kernel(x_ref, g_ref, w_ref, o_ref, hn_even_ref, hn_odd_ref):
    s = pl.program_id(0)

    @pl.when(s == 0)
    def _():
        hn_odd_ref[...] = jnp.zeros(hn_odd_ref.shape, BF16)

    def step(hn_prev_ref, hn_next_ref):
        o_ref[...] = jnp.dot(hn_prev_ref[...], w_ref[...], preferred_element_type=F32)
        hn_next_ref[...] = _rms_normalize(x_ref[...], g_ref[...]).astype(BF16)

    pl.when(s % 2 == 0)(functools.partial(step, hn_odd_ref, hn_even_ref))
    pl.when(s % 2 == 1)(functools.partial(step, hn_even_ref, hn_odd_ref))


def _norm_matmul(x, gain, w, tm):
    m, d = x.shape
    n = w.shape[1]
    n_blk = m // tm
    return pl.pallas_call(
        _norm_matmul_kernel,
        grid=(n_blk + 1,),
        in_specs=[
            pl.BlockSpec((tm, d), lambda s: (jnp.minimum(s, n_blk - 1), 0)),
            pl.BlockSpec((1, d), lambda s: (0, 0)),
            pl.BlockSpec((d, n), lambda s: (0, 0), pipeline_mode=pl.Buffered(1)),
        ],
        out_specs=pl.BlockSpec((tm, n), lambda s: (jnp.maximum(s - 1, 0), 0)),
        out_shape=jax.ShapeDtypeStruct((m, n), F32),
        scratch_shapes=[pltpu.VMEM((tm, d), BF16), pltpu.VMEM((tm, d), BF16)],
        compiler_params=_params("arbitrary"),
        name="norm_matmul",
    )(x, gain.reshape(1, d), w)


def _mix_ffn_kernel(*refs, n_in):
    x_ref = refs[0]
    a_refs = refs[1:1 + n_in]
    wo_refs = refs[1 + n_in:1 + 2 * n_in]
    g_ref, w1_ref, w3_ref, w2_ref, o_ref, hn_ref = refs[1 + 2 * n_in:]
    j = pl.program_id(1)

    @pl.when(j == 0)
    def _():
        x = x_ref[...]
        for a_ref, wo_ref in zip(a_refs, wo_refs):
            x = x + jnp.dot(a_ref[...], wo_ref[...], preferred_element_type=F32)
        hn_ref[...] = _rms_normalize(x, g_ref[...]).astype(BF16)
        o_ref[...] = x

    hn = hn_ref[...]
    a = jnp.dot(hn, w1_ref[...], preferred_element_type=F32)
    b = jnp.dot(hn, w3_ref[...], preferred_element_type=F32)
    act = (a * jax.nn.sigmoid(a) * b).astype(BF16)
    o_ref[...] += jnp.dot(act, w2_ref[...], preferred_element_type=F32)


def _mix_ffn(x, acts, wos, gain, w1, w3, w2, layer, tm, tf):
    m, d = x.shape
    f = w1.shape[2]
    n_in = len(acts)
    in_specs = [pl.BlockSpec((tm, d), lambda i, j: (i, 0))]
    in_specs += [pl.BlockSpec((tm, a.shape[1]), lambda i, j: (i, 0)) for a in acts]
    in_specs += [pl.BlockSpec(wo.shape, lambda i, j: (0, 0), pipeline_mode=pl.Buffered(1)) for wo in wos]
    in_specs += [
        pl.BlockSpec((1, d), lambda i, j: (0, 0)),
        pl.BlockSpec((None, d, tf), lambda i, j: (layer, 0, j)),
        pl.BlockSpec((None, d, tf), lambda i, j: (layer, 0, j)),
        pl.BlockSpec((None, tf, d), lambda i, j: (layer, j, 0)),
    ]
    return pl.pallas_call(
        functools.partial(_mix_ffn_kernel, n_in=n_in),
        grid=(m // tm, f // tf),
        in_specs=in_specs,
        out_specs=pl.BlockSpec((tm, d), lambda i, j: (i, 0)),
        out_shape=jax.ShapeDtypeStruct((m, d), F32),
        scratch_shapes=[pltpu.VMEM((tm, d), BF16)],
        compiler_params=_params("arbitrary", "arbitrary"),
        name="mix_ffn",
    )(x, *acts, *wos, gain.reshape(1, d), w1, w3, w2)


def _glu_kernel(y_ref, w_ref, b_ref, o_ref):
    g = jax.nn.gelu(y_ref[...])
    z = jnp.dot(g.astype(BF16), w_ref[...], preferred_element_type=F32) + b_ref[...]
    o_ref[...] = (g * jax.nn.sigmoid(z)).astype(o_ref.dtype)


def _glu(y, w, b, tm):
    m, c = y.shape
    return pl.pallas_call(
        _glu_kernel,
        grid=(m // tm,),
        in_specs=[
            pl.BlockSpec((tm, c), lambda i: (i, 0)),
            pl.BlockSpec((c, c), lambda i: (0, 0)),
            pl.BlockSpec((1, c), lambda i: (0, 0)),
        ],
        out_specs=pl.BlockSpec((tm, c), lambda i: (i, 0)),
        out_shape=jax.ShapeDtypeStruct((m, c), BF16),
        compiler_params=_params("arbitrary"),
        name="s5_glu",
    )(y, w, b.reshape(1, c))


def _s5_kernel(ut_ref, w_ref, m_ref, v_ref, a_ref, yt_ref, s_ref, hf_ref, hb_ref, *, n_seq, n_chunk):
    u = ut_ref[...].T.astype(BF16)
    s = jnp.dot(u, w_ref[...], preferred_element_type=F32)
    s_ref[0] = s[:, :S5_HALF]
    s_ref[1] = s[:, S5_HALF:]
    ar = a_ref[0:1, :]
    ai = a_ref[1:2, :]
    is_fwd = lax.broadcasted_iota(jnp.int32, (1, S5_HALF), 1) < S5_STATE

    def step(i, carry):
        hre, him = carry
        rows_f = pl.ds(i, n_seq, stride=n_chunk)
        rows_b = pl.ds(n_chunk - 1 - i, n_seq, stride=n_chunk)
        hf_ref[0, rows_f, :] = hre
        hf_ref[1, rows_f, :] = him
        hb_ref[0, rows_b, :] = hre
        hb_ref[1, rows_b, :] = him
        sre = jnp.where(is_fwd, s_ref[0, rows_f, :], s_ref[0, rows_b, :])
        sim = jnp.where(is_fwd, s_ref[1, rows_f, :], s_ref[1, rows_b, :])
        return ar * hre - ai * him + sre, ar * him + ai * hre + sim

    zero = jnp.zeros((n_seq, S5_HALF), F32)
    lax.fori_loop(0, n_chunk, step, (zero, zero), unroll=8)

    h_re = jnp.where(is_fwd, hf_ref[0], hb_ref[0])
    h_im = jnp.where(is_fwd, hf_ref[1], hb_ref[1])
    h = jnp.concatenate([h_re, h_im], axis=1).astype(BF16)
    y = (jnp.dot(u, m_ref[...], preferred_element_type=F32)
         + jnp.dot(h, v_ref[...], preferred_element_type=F32))
    yt_ref[...] = y.T


def _s5_scan(ut, w, m, v, a_t, n_seq):
    groups, vec, rows = ut.shape
    n_chunk = rows // n_seq
    return pl.pallas_call(
        functools.partial(_s5_kernel, n_seq=n_seq, n_chunk=n_chunk),
        grid=(groups,),
        in_specs=[
            pl.BlockSpec((None, vec, rows), lambda g: (g, 0, 0)),
            pl.BlockSpec((None, vec, 2 * S5_HALF), lambda g: (g, 0, 0)),
            pl.BlockSpec((None, vec, vec), lambda g: (g, 0, 0)),
            pl.BlockSpec((None, 2 * S5_HALF, vec), lambda g: (g, 0, 0)),
            pl.BlockSpec((None, 2, S5_HALF), lambda g: (g, 0, 0)),
        ],
        out_specs=pl.BlockSpec((None, vec, rows), lambda g: (g, 0, 0)),
        out_shape=jax.ShapeDtypeStruct((groups, vec, rows), F32),
        scratch_shapes=[
            pltpu.VMEM((2, rows, S5_HALF), F32),
            pltpu.VMEM((2, rows, S5_HALF), F32),
            pltpu.VMEM((2, rows, S5_HALF), F32),
        ],
        compiler_params=_params("arbitrary"),
        name="s5_scan",
    )(ut, w, m, v, a_t)


GROUPS_PER_LANE_BLOCK = LANES // S5_GROUP


def _to_groups_kernel(x_ref, o_ref):
    n_chunk = o_ref.shape[2]
    for tau in range(S5_CHUNK):
        rows = x_ref[pl.ds(tau, n_chunk, stride=S5_CHUNK), :]
        o_ref[:, tau * S5_GROUP:(tau + 1) * S5_GROUP, :] = rows.T.reshape(GROUPS_PER_LANE_BLOCK, S5_GROUP, n_chunk)


def _from_groups_kernel(y_ref, o_ref):
    n_chunk = y_ref.shape[2]
    for tau in range(S5_CHUNK):
        slab = y_ref[:, tau * S5_GROUP:(tau + 1) * S5_GROUP, :].reshape(LANES, n_chunk)
        o_ref[pl.ds(tau, n_chunk, stride=S5_CHUNK), :] = slab.T


def _s5_to_groups(x, width, chunks_per_step):
    tokens = x.shape[0]
    n_chunk = tokens // S5_CHUNK
    return pl.pallas_call(
        _to_groups_kernel,
        grid=(n_chunk // chunks_per_step, width // LANES),
        in_specs=[pl.BlockSpec((chunks_per_step * S5_CHUNK, LANES), lambda i, c: (i, c))],
        out_specs=pl.BlockSpec((GROUPS_PER_LANE_BLOCK, S5_VEC, chunks_per_step), lambda i, c: (c, 0, i)),
        out_shape=jax.ShapeDtypeStruct((width // S5_GROUP, S5_VEC, n_chunk), F32),
        compiler_params=_params("arbitrary", "arbitrary"),
        name="s5_to_groups",
    )(x)


def _s5_from_groups(yt, chunks_per_step):
    groups, _, n_chunk = yt.shape
    width = groups * S5_GROUP
    return pl.pallas_call(
        _from_groups_kernel,
        grid=(n_chunk // chunks_per_step, width // LANES),
        in_specs=[pl.BlockSpec((GROUPS_PER_LANE_BLOCK, S5_VEC, chunks_per_step), lambda i, c: (c, 0, i))],
        out_specs=pl.BlockSpec((chunks_per_step * S5_CHUNK, LANES), lambda i, c: (i, c)),
        out_shape=jax.ShapeDtypeStruct((n_chunk * S5_CHUNK, width), F32),
        compiler_params=_params("arbitrary", "arbitrary"),
        name="s5_from_groups",
    )(yt)


def _s5_operators(lam_re, lam_im, log_dt, b_re, b_im, c_re, c_im, d_skip):
    hi = lax.Precision.HIGHEST
    t = S5_CHUNK
    groups = lam_re.shape[1]
    lr = jnp.minimum(lam_re.astype(F32), -1e-4)
    li = lam_im.astype(F32)
    dt = jnp.exp(log_dt.astype(F32))[..., None]
    mag = jnp.exp(lr * dt)
    ab_re = mag * jnp.cos(li * dt)
    ab_im = mag * jnp.sin(li * dt)
    nr, ni = ab_re - 1.0, ab_im
    den = lr * lr + li * li
    f_re = (nr * lr + ni * li) / den
    f_im = (ni * lr - nr * li) / den
    br, bi = b_re.astype(F32), b_im.astype(F32)
    bb_re = f_re[..., None] * br - f_im[..., None] * bi
    bb_im = f_re[..., None] * bi + f_im[..., None] * br
    cr, ci = c_re.astype(F32), c_im.astype(F32)

    k = jnp.arange(t + 1, dtype=F32)
    mag_k = jnp.exp((lr * dt)[..., None] * k)
    ang_k = (li * dt)[..., None] * k
    pr = mag_k * jnp.cos(ang_k)
    pi = mag_k * jnp.sin(ang_k)

    ab_r = pr[..., :t, None] * bb_re[..., None, :] - pi[..., :t, None] * bb_im[..., None, :]
    ab_i = pr[..., :t, None] * bb_im[..., None, :] + pi[..., :t, None] * bb_re[..., None, :]
    lag = (jnp.einsum('zgnp,zgpkm->zgknm', cr, ab_r, precision=hi)
           - jnp.einsum('zgnp,zgpkm->zgknm', ci, ab_i, precision=hi))

    k_idx = np.arange(t)[:, None, None]
    s_idx = np.arange(t)[None, :, None]
    t_idx = np.arange(t)[None, None, :]
    sel_f = jnp.asarray((t_idx - s_idx == k_idx).astype(np.float32))
    sel_b = jnp.asarray((s_idx - t_idx == k_idx).astype(np.float32))
    skip = jnp.einsum('st,gn,nm->gsmtn', jnp.eye(t, dtype=F32), d_skip.astype(F32).reshape(groups, S5_GROUP),
                      jnp.eye(S5_GROUP, dtype=F32), precision=hi)
    m_op = (jnp.einsum('kst,gknm->gsmtn', sel_f, lag[0])
            + jnp.einsum('kst,gknm->gsmtn', sel_b, lag[1])
            + skip).reshape(groups, S5_VEC, S5_VEC)

    pf_r, pf_i = pr[0][..., :t][..., ::-1], pi[0][..., :t][..., ::-1]
    pb_r, pb_i = pr[1][..., :t], pi[1][..., :t]

    def contrib(p_r, p_i, z):
        re = p_r[..., None] * bb_re[z][:, :, None, :] - p_i[..., None] * bb_im[z][:, :, None, :]
        im = p_r[..., None] * bb_im[z][:, :, None, :] + p_i[..., None] * bb_re[z][:, :, None, :]
        to_rows = lambda x: x.transpose(0, 2, 3, 1).reshape(groups, S5_VEC, S5_STATE)
        return to_rows(re), to_rows(im)

    wf_re, wf_im = contrib(pf_r, pf_i, 0)
    wb_re, wb_im = contrib(pb_r, pb_i, 1)
    w_op = jnp.concatenate([wf_re, wb_re, wf_im, wb_im], axis=-1)

    qf_r, qf_i = pr[0][..., 1:], pi[0][..., 1:]
    qb_r, qb_i = pr[1][..., 1:][..., ::-1], pi[1][..., 1:][..., ::-1]

    def readout(q_r, q_i, z):
        c_r = cr[z].transpose(0, 2, 1)[:, :, None, :]
        c_i = ci[z].transpose(0, 2, 1)[:, :, None, :]
        w_r = c_r * q_r[..., None] - c_i * q_i[..., None]
        w_i = c_r * q_i[..., None] + c_i * q_r[..., None]
        return w_r.reshape(groups, S5_STATE, S5_VEC), (-w_i).reshape(groups, S5_STATE, S5_VEC)

    vf_re, vf_im = readout(qf_r, qf_i, 0)
    vb_re, vb_im = readout(qb_r, qb_i, 1)
    v_op = jnp.concatenate([vf_re, vb_re, vf_im, vb_im], axis=1)

    a_t = jnp.stack([jnp.concatenate([pr[0][..., t], pr[1][..., t]], axis=-1),
                     jnp.concatenate([pi[0][..., t], pi[1][..., t]], axis=-1)], axis=1)
    return w_op.astype(BF16), m_op.astype(BF16), v_op.astype(BF16), a_t


def _rope(x, cos, sin_lo, sin_hi):
    return (x * cos
            + pltpu.roll(x, HEAD_DIM - ROPE_HALF, 1) * sin_lo
            + pltpu.roll(x, ROPE_HALF, 1) * sin_hi)


def _attn_kernel(sink_ref, q_ref, k_ref, v_ref, qn_ref, kn_ref, cos_ref, slo_ref, shi_ref,
                 o_ref, kb_ref, vb_ref, *, seq, blocks_per_step):
    h = pl.program_id(1)
    n = pl.program_id(2)
    prep_rows = 256

    @pl.when(n == 0)
    def _():
        pad = jnp.zeros((BLOCK, HEAD_DIM), BF16)
        kb_ref[0:BLOCK, :] = pad
        vb_ref[0:BLOCK, :] = pad
        kb_ref[BLOCK + seq:, :] = pad
        vb_ref[BLOCK + seq:, :] = pad

        def prep(c, carry):
            r0 = pl.multiple_of(c * prep_rows, prep_rows)
            rows = pl.ds(r0, prep_rows)
            k = _rms_normalize(k_ref[rows, :], kn_ref[...])
            k = _rope(k, cos_ref[rows, :], slo_ref[rows, :], shi_ref[rows, :])
            kb_ref[pl.ds(r0 + BLOCK, prep_rows), :] = k.astype(BF16)
            vb_ref[pl.ds(r0 + BLOCK, prep_rows), :] = v_ref[rows, :].astype(BF16)
            return carry

        lax.fori_loop(0, seq // prep_rows, prep, 0)

    qi = lax.broadcasted_iota(jnp.int32, (BLOCK, 1), 0)
    kj = lax.broadcasted_iota(jnp.int32, (BLOCK, 3 * BLOCK), 1)

    def query_block(qb, carry):
        local = pl.multiple_of(qb * BLOCK, BLOCK)
        r0 = pl.multiple_of((n * blocks_per_step + qb) * BLOCK, BLOCK)
        rows = pl.ds(r0, BLOCK)
        cos, slo, shi = cos_ref[rows, :], slo_ref[rows, :], shi_ref[rows, :]
        heads = []
        for g in range(Q_PER_KV):
            qg = _rms_normalize(q_ref[pl.ds(local, BLOCK), g * HEAD_DIM:(g + 1) * HEAD_DIM], qn_ref[...])
            heads.append(_rope(qg, cos, slo, shi).astype(BF16))
        q4 = jnp.concatenate(heads, axis=0)
        kw = kb_ref[pl.ds(r0, 3 * BLOCK), :]
        vw = vb_ref[pl.ds(r0, 3 * BLOCK), :]
        s = lax.dot_general(q4, kw, (((1,), (1,)), ((), ())), preferred_element_type=F32)
        s = s * (HEAD_DIM ** -0.5)

        lo = jnp.maximum(qi + (BLOCK - WINDOW), BLOCK - r0)
        hi = jnp.minimum(qi + (BLOCK + WINDOW), seq - 1 + BLOCK - r0)
        probs = []
        inv_denoms = []
        for g in range(Q_PER_KV):
            sg = s[g * BLOCK:(g + 1) * BLOCK, :]
            sg = jnp.where(kj >= lo, jnp.where(kj <= hi, sg, NEG_INF), NEG_INF)
            sk = sink_ref[h * Q_PER_KV + g]
            mx = jnp.maximum(jnp.max(sg, axis=-1, keepdims=True), sk)
            p = jnp.exp(sg - mx)
            denom = jnp.sum(p, axis=-1, keepdims=True) + jnp.exp(sk - mx)
            probs.append(p.astype(BF16))
            inv_denoms.append(1.0 / denom)
        p4 = jnp.concatenate(probs, axis=0)
        o4 = jnp.dot(p4, vw, preferred_element_type=F32)
        for g in range(Q_PER_KV):
            og = o4[g * BLOCK:(g + 1) * BLOCK, :] * inv_denoms[g]
            o_ref[pl.ds(local, BLOCK), g * HEAD_DIM:(g + 1) * HEAD_DIM] = og.astype(o_ref.dtype)
        return carry

    lax.fori_loop(0, blocks_per_step, query_block, 0)


def _attention(proj, sink, q_norm, k_norm, cos, sin_lo, sin_hi, q_col0, k_col0, v_col0, blocks_per_step):
    bsz, seq, _ = proj.shape
    q_w = Q_PER_KV * HEAD_DIM
    q_blk0, k_blk0, v_blk0 = q_col0 // q_w, k_col0 // HEAD_DIM, v_col0 // HEAD_DIM
    table = pl.BlockSpec((seq, HEAD_DIM), lambda b, h, n: (0, 0))
    step_rows = blocks_per_step * BLOCK
    return pl.pallas_call(
        functools.partial(_attn_kernel, seq=seq, blocks_per_step=blocks_per_step),
        grid=(bsz, N_KV_HEADS, seq // step_rows),
        in_specs=[
            pl.BlockSpec(memory_space=pltpu.SMEM),
            pl.BlockSpec((None, step_rows, q_w), lambda b, h, n: (b, n, q_blk0 + h)),
            pl.BlockSpec((None, seq, HEAD_DIM), lambda b, h, n: (b, 0, k_blk0 + h)),
            pl.BlockSpec((None, seq, HEAD_DIM), lambda b, h, n: (b, 0, v_blk0 + h)),
            pl.BlockSpec((1, HEAD_DIM), lambda b, h, n: (0, 0)),
            pl.BlockSpec((1, HEAD_DIM), lambda b, h, n: (0, 0)),
            table, table, table,
        ],
        out_specs=pl.BlockSpec((None, step_rows, q_w), lambda b, h, n: (b, n, h)),
        out_shape=jax.ShapeDtypeStruct((bsz, seq, N_KV_HEADS * q_w), BF16),
        scratch_shapes=[
            pltpu.VMEM((seq + 2 * BLOCK, HEAD_DIM), BF16),
            pltpu.VMEM((seq + 2 * BLOCK, HEAD_DIM), BF16),
        ],
        compiler_params=_params("arbitrary", "arbitrary", "arbitrary"),
        name="window_attention",
    )(sink, proj, proj, proj, q_norm.reshape(1, HEAD_DIM), k_norm.reshape(1, HEAD_DIM), cos, sin_lo, sin_hi)


def _rope_tables(seq):
    inv = ROPE_THETA ** (-jnp.arange(ROPE_HALF, dtype=F32) / ROPE_HALF)
    ang = jnp.arange(seq).astype(F32)[:, None] * inv[None, :]
    cos, sin = jnp.cos(ang), jnp.sin(ang)
    rest = HEAD_DIM - 2 * ROPE_HALF
    cos_t = jnp.concatenate([cos, cos, jnp.ones((seq, rest), F32)], axis=-1)
    sin_lo = jnp.concatenate([-sin, jnp.zeros((seq, HEAD_DIM - ROPE_HALF), F32)], axis=-1)
    sin_hi = jnp.concatenate([jnp.zeros((seq, ROPE_HALF), F32), sin, jnp.zeros((seq, rest), F32)], axis=-1)
    return cos_t, sin_lo, sin_hi


def _rglru_kernel(gate_ref, xb_ref, cw_ref, cb_ref, wf_ref, wb_ref, bias_ref, lam_ref, o_ref,
                  xi_ref, conv_ref, hf_ref, pf_ref, hb_ref, pb_ref, af_ref, bf_ref, ab_ref, bb_ref, *, seq, tch):
    seg_len = seq // SUBLANES
    n_slab = LRU_BS // LANES
    halo = CONV_LEFT * SUBLANES
    n_chunk = seg_len // tch
    chunk_rows = tch * SUBLANES

    def interleave(tb, carry):
        t0 = tb * SUBLANES
        for s in range(SUBLANES):
            tile = xb_ref[pl.ds(pl.multiple_of(s * seg_len + t0, SUBLANES), SUBLANES), :]
            dst = pl.ds(halo + t0 * SUBLANES + s, SUBLANES, stride=SUBLANES)
            for slab in range(n_slab):
                xi_ref[slab, dst, :] = tile[:, slab * LANES:(slab + 1) * LANES]
        return carry

    lax.fori_loop(0, seg_len // SUBLANES, interleave, 0)

    sub = lax.broadcasted_iota(jnp.int32, (SUBLANES, LANES), 0)
    for slab in range(n_slab):
        for back in range(1, CONV_LEFT + 1):
            src = halo + (seg_len - back) * SUBLANES
            tile = pltpu.roll(xi_ref[slab, src:src + SUBLANES, :], 1, 0)
            xi_ref[slab, halo - back * SUBLANES:halo - (back - 1) * SUBLANES, :] = jnp.where(sub == 0, 0.0, tile)
        tile = pltpu.roll(xi_ref[slab, halo:halo + SUBLANES, :], SUBLANES - 1, 0)
        end = halo + seq
        xi_ref[slab, end:end + SUBLANES, :] = jnp.where(sub == SUBLANES - 1, 0.0, tile)

    cw = cw_ref[...]
    cb = cb_ref[...]

    def conv_chunk(c, carry):
        t0 = c * tch
        parts = []
        for slab in range(n_slab):
            cols = slice(slab * LANES, (slab + 1) * LANES)
            acc = cb[:, cols]
            for tap in range(CONV_WIDTH):
                r0 = pl.multiple_of((t0 + tap) * SUBLANES, SUBLANES)
                acc = acc + cw[tap:tap + 1, cols] * xi_ref[slab, pl.ds(r0, chunk_rows), :]
            parts.append(acc)
        conv_ref[pl.ds(pl.multiple_of(t0 * SUBLANES, SUBLANES), chunk_rows), :] = jnp.concatenate(parts, axis=1)
        return carry

    lax.fori_loop(0, n_chunk, conv_chunk, 0)

    def softplus(x):
        return jnp.maximum(x, 0.0) + jnp.log1p(jnp.exp(-jnp.abs(x)))

    def gates(t0, w_ref, z, a_out, b_out):
        half_conv = conv_ref[pl.ds(pl.multiple_of(t0 * SUBLANES, SUBLANES), chunk_rows), :]
        pre = jnp.dot(half_conv.astype(BF16), w_ref[...], preferred_element_type=F32)
        t_r = jnp.tanh(pre[:, :LRU_BS] + bias_ref[2 * z:2 * z + 1, :])
        t_i = jnp.tanh(pre[:, LRU_BS:] + bias_ref[2 * z + 1:2 * z + 2, :])
        c1 = (-0.5 * LRU_C) * softplus(-lam_ref[z:z + 1, :])
        a = jnp.exp(c1 * t_r + c1)
        a_out[...] = a
        v = 1.0 - a * a
        root = jnp.where(v > 0.0, v * lax.rsqrt(v), 0.0)
        b_out[...] = root * (half_conv * t_i + half_conv)

    def chunk_pair(c, carry):
        tf0 = c * tch
        tb0 = (n_chunk - 1 - c) * tch
        gates(tf0, wf_ref, 0, af_ref, bf_ref)
        gates(tb0, wb_ref, 1, ab_ref, bb_ref)

        def steps(k, state):
            hf, pf, hb, pb = state
            steps_per_trip = SUBLANES
            group = steps_per_trip * SUBLANES
            src_f = pl.ds(pl.multiple_of(k * group, group), group)
            src_b = pl.ds(pl.multiple_of(chunk_rows - group - k * group, group), group)
            dst_f = pl.ds(pl.multiple_of(tf0 * SUBLANES + k * group, group), group)
            dst_b = pl.ds(pl.multiple_of(tb0 * SUBLANES + chunk_rows - group - k * group, group), group)
            a_f, b_f = af_ref[src_f, :], bf_ref[src_f, :]
            a_b, b_b = ab_ref[src_b, :], bb_ref[src_b, :]
            hfs, pfs, hbs, pbs = [], [], [], []
            for j in range(steps_per_trip):
                up = slice(j * SUBLANES, (j + 1) * SUBLANES)
                down = slice((steps_per_trip - 1 - j) * SUBLANES, (steps_per_trip - j) * SUBLANES)
                hf = a_f[up] * hf + b_f[up]
                pf = a_f[up] * pf
                hb = a_b[down] * hb + b_b[down]
                pb = a_b[down] * pb
                hfs.append(hf)
                pfs.append(pf)
                hbs.insert(0, hb)
                pbs.insert(0, pb)
            hf_ref[dst_f, :] = jnp.concatenate(hfs, axis=0)
            pf_ref[dst_f, :] = jnp.concatenate(pfs, axis=0)
            hb_ref[dst_b, :] = jnp.concatenate(hbs, axis=0)
            pb_ref[dst_b, :] = jnp.concatenate(pbs, axis=0)
            return hf, pf, hb, pb

        return lax.fori_loop(0, tch // SUBLANES, steps, carry)

    zeros = jnp.zeros((SUBLANES, LRU_BS), F32)
    ones = jnp.ones((SUBLANES, LRU_BS), F32)
    end_f, prod_f, end_b, prod_b = lax.fori_loop(0, n_chunk, chunk_pair, (zeros, ones, zeros, ones))

    cin_f = [jnp.zeros((1, LRU_BS), F32)]
    for s in range(SUBLANES - 1):
        cin_f.append(end_f[s:s + 1, :] + prod_f[s:s + 1, :] * cin_f[s])
    cin_b = [jnp.zeros((1, LRU_BS), F32)]
    for s in range(SUBLANES - 1, 0, -1):
        cin_b.insert(0, end_b[s:s + 1, :] + prod_b[s:s + 1, :] * cin_b[0])
    def per_chunk(rows):
        tile = _rows_to_tile(rows, reverse=False)
        return jnp.broadcast_to(tile[None], (tch, SUBLANES, LRU_BS)).reshape(chunk_rows, LRU_BS)

    cin_f_tiles = per_chunk(cin_f)
    cin_b_tiles = per_chunk(cin_b)

    def fix_up(c, carry):
        rows = pl.ds(pl.multiple_of(c * chunk_rows, chunk_rows), chunk_rows)
        total = (hf_ref[rows, :] + pf_ref[rows, :] * cin_f_tiles) + (hb_ref[rows, :] + pb_ref[rows, :] * cin_b_tiles)
        for slab in range(n_slab):
            xi_ref[slab, rows, :] = total[:, slab * LANES:(slab + 1) * LANES]
        return carry

    lax.fori_loop(0, n_chunk, fix_up, 0)

    out_rows = 2 * SUBLANES

    def finish(tb, carry):
        for s in range(SUBLANES):
            halves = []
            for half in range(2):
                t0 = tb * out_rows + half * SUBLANES
                src = pl.ds(t0 * SUBLANES + s, SUBLANES, stride=SUBLANES)
                halves.append(jnp.concatenate([xi_ref[slab, src, :] for slab in range(n_slab)], axis=1))
            rows = pl.ds(pl.multiple_of(s * seg_len + tb * out_rows, out_rows), out_rows)
            y = jnp.concatenate(halves, axis=0) * jax.nn.gelu(gate_ref[rows, :])
            o_ref[rows, :] = y.astype(o_ref.dtype)
        return carry

    lax.fori_loop(0, seg_len // out_rows, finish, 0)


def _rglru(proj, conv_w, conv_b, w_fwd, w_bwd, bias, lam, tch):
    bsz, seq, width2 = proj.shape
    n_blk = width2 // (2 * LRU_BS)
    n_slab = LRU_BS // LANES
    seq_blk = lambda off: pl.BlockSpec((None, seq, LRU_BS), lambda b, n: (b, 0, off + n))
    per_blk = lambda rows, cols: pl.BlockSpec((None, rows, cols), lambda b, n: (n, 0, 0))
    slab_rows = pltpu.VMEM((seq, LRU_BS), F32)
    gate_rows = pltpu.VMEM((tch * SUBLANES, LRU_BS), F32)
    return pl.pallas_call(
        functools.partial(_rglru_kernel, seq=seq, tch=tch),
        grid=(bsz, n_blk),
        in_specs=[
            seq_blk(0),
            seq_blk(n_blk),
            pl.BlockSpec((CONV_WIDTH, LRU_BS), lambda b, n: (0, n)),
            pl.BlockSpec((1, LRU_BS), lambda b, n: (0, n)),
            per_blk(LRU_BS, 2 * LRU_BS),
            per_blk(LRU_BS, 2 * LRU_BS),
            per_blk(4, LRU_BS),
            per_blk(2, LRU_BS),
        ],
        out_specs=pl.BlockSpec((None, seq, LRU_BS), lambda b, n: (b, 0, n)),
        out_shape=jax.ShapeDtypeStruct((bsz, seq, n_blk * LRU_BS), BF16),
        scratch_shapes=[
            pltpu.VMEM((n_slab, seq + (CONV_WIDTH - 1) * SUBLANES, LANES), F32),
            pltpu.VMEM((seq, LRU_BS), F32),
            slab_rows, slab_rows, slab_rows, slab_rows,
            gate_rows, gate_rows, gate_rows, gate_rows,
        ],
        compiler_params=_params("arbitrary", "arbitrary"),
        name="rglru",
    )(proj, proj, conv_w, conv_b.reshape(1, -1), w_fwd, w_bwd, bias, lam)


def _even_layer(x, p, bsz, seq):
    m, d = x.shape
    s5_width = p["s5_w_glu"].shape[0]
    attn_width = N_KV_HEADS * Q_PER_KV * HEAD_DIM
    kv_width = N_KV_HEADS * HEAD_DIM
    proj = _norm_matmul(x, p["norm_mix"], p["w_in"], tm=ROW_TILE)

    ut = _s5_to_groups(proj, s5_width, chunks_per_step=RELAYOUT_CHUNKS)
    yt = _s5_scan(ut, p["s5_w"], p["s5_m"], p["s5_v"], p["s5_a"], n_seq=bsz)
    y = _s5_from_groups(yt, chunks_per_step=RELAYOUT_CHUNKS)
    y_s5 = _glu(y, p["s5_w_glu"], p["s5_b_glu"], tm=ROW_TILE)

    y_attn = _attention(proj.reshape(bsz, seq, -1), p["attn_sink"], p["attn_q_norm"], p["attn_k_norm"],
                        p["rope_cos"], p["rope_sin_lo"], p["rope_sin_hi"],
                        q_col0=s5_width, k_col0=s5_width + attn_width,
                        v_col0=s5_width + attn_width + kv_width, blocks_per_step=ATTN_BLOCKS_PER_STEP)
    return [y_s5, y_attn.reshape(m, attn_width)], [p["w_out"][:s5_width], p["w_out"][s5_width:]]


def _odd_layer(x, p, bsz, seq):
    m, d = x.shape
    proj = _norm_matmul(x, p["norm_mix"], p["w_in"], tm=ROW_TILE)
    y = _rglru(proj.reshape(bsz, seq, -1), p["conv_w"], p["conv_b"], p["w_fwd"], p["w_bwd"],
               p["gate_bias"], p["lam"], tch=LRU_TIME_CHUNK)
    return [y.reshape(m, -1)], [p["w_out"]]


def kernel(x_prompt, x_sample, norm_mix, norm_ffn, ev_w_in, ev_w_out, s5_lam_re, s5_lam_im, s5_log_dt, s5_b_re, s5_b_im, s5_c_re, s5_c_im, s5_d, s5_w_glu, s5_b_glu, attn_q_norm, attn_k_norm, attn_sink, od_w_in, od_w_out, lru_conv_w, lru_conv_b, lru_wa, lru_ba, lru_wx, lru_bx, lru_lam, ffn_w1, ffn_w3, ffn_w2):
    depth = norm_mix.shape[0]
    seq_lens = {x_prompt.shape[1], x_sample.shape[1]}
    rope = {s: _rope_tables(s) for s in seq_lens}

    layers = []
    for layer in range(depth):
        if layer % 2 == 0:
            e = layer // 2
            s5_w, s5_m, s5_v, s5_a = _s5_operators(s5_lam_re[e], s5_lam_im[e], s5_log_dt[e], s5_b_re[e],
                                                   s5_b_im[e], s5_c_re[e], s5_c_im[e], s5_d[e])
            mix = dict(norm_mix=norm_mix[layer], w_in=ev_w_in[e].astype(BF16), w_out=ev_w_out[e].astype(BF16),
                       s5_w=s5_w, s5_m=s5_m, s5_v=s5_v, s5_a=s5_a,
                       s5_w_glu=s5_w_glu[e].astype(BF16), s5_b_glu=s5_b_glu[e],
                       attn_q_norm=attn_q_norm[e], attn_k_norm=attn_k_norm[e], attn_sink=attn_sink[e])
        else:
            o = layer // 2
            n_blk = lru_wa.shape[2]
            mix = dict(norm_mix=norm_mix[layer], w_in=od_w_in[o].astype(BF16), w_out=od_w_out[o].astype(BF16),
                       conv_w=0.5 * lru_conv_w[o], conv_b=0.5 * lru_conv_b[o],
                       w_fwd=jnp.concatenate([lru_wa[o, 0], lru_wx[o, 0]], axis=-1).astype(BF16),
                       w_bwd=jnp.concatenate([lru_wa[o, 1], lru_wx[o, 1]], axis=-1).astype(BF16),
                       gate_bias=(0.5 * jnp.stack([lru_ba[o, 0], lru_bx[o, 0], lru_ba[o, 1], lru_bx[o, 1]], axis=0))
                       .reshape(4, n_blk, LRU_BS).transpose(1, 0, 2),
                       lam=lru_lam[o].reshape(2, n_blk, LRU_BS).transpose(1, 0, 2))
        layers.append(mix)
    w1_all, w3_all, w2_all = ffn_w1.astype(BF16), ffn_w3.astype(BF16), ffn_w2.astype(BF16)

    def run(x):
        bsz, seq, d = x.shape
        h = x.reshape(bsz * seq, d)
        for layer, mix in enumerate(layers):
            if layer % 2 == 0:
                cos, sin_lo, sin_hi = rope[seq]
                acts, wos = _even_layer(h, dict(mix, rope_cos=cos, rope_sin_lo=sin_lo, rope_sin_hi=sin_hi), bsz, seq)
            else:
                acts, wos = _odd_layer(h, mix, bsz, seq)
            h = _mix_ffn(h, acts, wos, norm_ffn[layer], w1_all, w3_all, w2_all, layer, tm=ROW_TILE, tf=FF_TILE)
        return h.reshape(bsz, seq, d)

    return (run(x_prompt), run(x_sample))
```

```python
import functools

import jax
import jax.numpy as jnp
import numpy as np
from jax import lax
from jax.experimental import pallas as pl
from jax.experimental.pallas import tpu as pltpu

F32 = jnp.float32
BF16 = jnp.bfloat16

LANES = 128
SUBLANES = 8
VMEM_LIMIT_BYTES = 56 * 1024 * 1024

EPS = 1e-6
NEG_INF = -1e30
S5_GROUP = 16
S5_STATE = 64
HEAD_DIM = 128
N_KV_HEADS = 2
Q_PER_KV = 4
WINDOW = 128
BLOCK = 128
ROPE_HALF = 16
ROPE_THETA = 500000.0
LRU_BS = 256
LRU_C = 8.0
CONV_WIDTH = 4
CONV_LEFT = 2

ROW_TILE = 512
FF_TILE = 512
RELAYOUT_CHUNKS = 512
ATTN_BLOCKS_PER_STEP = 16
LRU_TIME_CHUNK = 128

S5_CHUNK = 16
S5_VEC = S5_CHUNK * S5_GROUP
S5_HALF = 2 * S5_STATE


def _params(*semantics):
    return pltpu.CompilerParams(dimension_semantics=semantics, vmem_limit_bytes=VMEM_LIMIT_BYTES)


def _rows_to_tile(rows, reverse):
    n = rows[0].shape[1]
    sub = lax.broadcasted_iota(jnp.int32, (SUBLANES, n), 0)
    tile = jnp.broadcast_to(rows[0], (SUBLANES, n))
    for j in range(1, SUBLANES):
        tile = jnp.where(sub == (SUBLANES - 1 - j if reverse else j), jnp.broadcast_to(rows[j], (SUBLANES, n)), tile)
    return tile


def _rms_normalize(x, gain):
    ms = jnp.mean(x * x, axis=-1, keepdims=True)
    return x * lax.rsqrt(ms + EPS) * gain


def _norm_matmul_kernel(x_ref, g_ref, w_ref, o_ref, hn_even_ref, hn_odd_ref):
    s = pl.program_id(0)

    @pl.when(s == 0)
    def _():
        hn_odd_ref[...] = jnp.zeros(hn_odd_ref.shape, BF16)

    def step(hn_prev_ref, hn_next_ref):
        o_ref[...] = jnp.dot(hn_prev_ref[...], w_ref[...], preferred_element_type=F32)
        hn_next_ref[...] = _rms_normalize(x_ref[...], g_ref[...]).astype(BF16)

    pl.when(s % 2 == 0)(functools.partial(step, hn_odd_ref, hn_even_ref))
    pl.when(s % 2 == 1)(functools.partial(step, hn_even_ref, hn_odd_ref))


def _norm_matmul(x, gain, w, tm):
    m, d = x.shape
    n = w.shape[1]
    n_blk = m // tm
    return pl.pallas_call(
        _norm_matmul_kernel,
        grid=(n_blk + 1,),
        in_specs=[
            pl.BlockSpec((tm, d), lambda s: (jnp.minimum(s, n_blk - 1), 0)),
            pl.BlockSpec((1, d), lambda s: (0, 0)),
            pl.BlockSpec((d, n), lambda s: (0, 0), pipeline_mode=pl.Buffered(1)),
        ],
        out_specs=pl.BlockSpec((tm, n), lambda s: (jnp.maximum(s - 1, 0), 0)),
        out_shape=jax.ShapeDtypeStruct((m, n), F32),
        scratch_shapes=[pltpu.VMEM((tm, d), BF16), pltpu.VMEM((tm, d), BF16)],
        compiler_params=_params("arbitrary"),
        name="norm_matmul",
    )(x, gain.reshape(1, d), w)


def _mix_ffn_kernel(*refs, n_in):
    x_ref = refs[0]
    a_refs = refs[1:1 + n_in]
    wo_refs = refs[1 + n_in:1 + 2 * n_in]
    g_ref, w1_ref, w3_ref, w2_ref, o_ref, hn_ref = refs[1 + 2 * n_in:]
    j = pl.program_id(1)

    @pl.when(j == 0)
    def _():
        x = x_ref[...]
        for a_ref, wo_ref in zip(a_refs, wo_refs):
            x = x + jnp.dot(a_ref[...], wo_ref[...], preferred_element_type=F32)
        hn_ref[...] = _rms_normalize(x, g_ref[...]).astype(BF16)
        o_ref[...] = x

    hn = hn_ref[...]
    a = jnp.dot(hn, w1_ref[...], preferred_element_type=F32)
    b = jnp.dot(hn, w3_ref[...], preferred_element_type=F32)
    act = (a * jax.nn.sigmoid(a) * b).astype(BF16)
    o_ref[...] += jnp.dot(act, w2_ref[...], preferred_element_type=F32)


def _mix_ffn(x, acts, wos, gain, w1, w3, w2, layer, tm, tf):
    m, d = x.shape
    f = w1.shape[2]
    n_in = len(acts)
    in_specs = [pl.BlockSpec((tm, d), lambda i, j: (i, 0))]
    in_specs += [pl.BlockSpec((tm, a.shape[1]), lambda i, j: (i, 0)) for a in acts]
    in_specs += [pl.BlockSpec(wo.shape, lambda i, j: (0, 0), pipeline_mode=pl.Buffered(1)) for wo in wos]
    in_specs += [
        pl.BlockSpec((1, d), lambda i, j: (0, 0)),
        pl.BlockSpec((None, d, tf), lambda i, j: (layer, 0, j)),
        pl.BlockSpec((None, d, tf), lambda i, j: (layer, 0, j)),
        pl.BlockSpec((None, tf, d), lambda i, j: (layer, j, 0)),
    ]
    return pl.pallas_call(
        functools.partial(_mix_ffn_kernel, n_in=n_in),
        grid=(m // tm, f // tf),
        in_specs=in_specs,
        out_specs=pl.BlockSpec((tm, d), lambda i, j: (i, 0)),
        out_shape=jax.ShapeDtypeStruct((m, d), F32),
        scratch_shapes=[pltpu.VMEM((tm, d), BF16)],
        compiler_params=_params("arbitrary", "arbitrary"),
        name="mix_ffn",
    )(x, *acts, *wos, gain.reshape(1, d), w1, w3, w2)


def _glu_kernel(y_ref, w_ref, b_ref, o_ref):
    g = jax.nn.gelu(y_ref[...])
    z = jnp.dot(g.astype(BF16), w_ref[...], preferred_element_type=F32) + b_ref[...]
    o_ref[...] = (g * jax.nn.sigmoid(z)).astype(o_ref.dtype)


def _glu(y, w, b, tm):
    m, c = y.shape
    return pl.pallas_call(
        _glu_kernel,
        grid=(m // tm,),
        in_specs=[
            pl.BlockSpec((tm, c), lambda i: (i, 0)),
            pl.BlockSpec((c, c), lambda i: (0, 0)),
            pl.BlockSpec((1, c), lambda i: (0, 0)),
        ],
        out_specs=pl.BlockSpec((tm, c), lambda i: (i, 0)),
        out_shape=jax.ShapeDtypeStruct((m, c), BF16),
        compiler_params=_params("arbitrary"),
        name="s5_glu",
    )(y, w, b.reshape(1, c))


def _s5_kernel(ut_ref, w_ref, m_ref, v_ref, a_ref, yt_ref, s_ref, hf_ref, hb_ref, *, n_seq, n_chunk):
    u = ut_ref[...].T.astype(BF16)
    s = jnp.dot(u, w_ref[...], preferred_element_type=F32)
    s_ref[0] = s[:, :S5_HALF]
    s_ref[1] = s[:, S5_HALF:]
    ar = a_ref[0:1, :]
    ai = a_ref[1:2, :]
    is_fwd = lax.broadcasted_iota(jnp.int32, (1, S5_HALF), 1) < S5_STATE

    def step(i, carry):
        hre, him = carry
        rows_f = pl.ds(i, n_seq, stride=n_chunk)
        rows_b = pl.ds(n_chunk - 1 - i, n_seq, stride=n_chunk)
        hf_ref[0, rows_f, :] = hre
        hf_ref[1, rows_f, :] = him
        hb_ref[0, rows_b, :] = hre
        hb_ref[1, rows_b, :] = him
        sre = jnp.where(is_fwd, s_ref[0, rows_f, :], s_ref[0, rows_b, :])
        sim = jnp.where(is_fwd, s_ref[1, rows_f, :], s_ref[1, rows_b, :])
        return ar * hre - ai * him + sre, ar * him + ai * hre + sim

    zero = jnp.zeros((n_seq, S5_HALF), F32)
    lax.fori_loop(0, n_chunk, step, (zero, zero), unroll=8)

    h_re = jnp.where(is_fwd, hf_ref[0], hb_ref[0])
    h_im = jnp.where(is_fwd, hf_ref[1], hb_ref[1])
    h = jnp.concatenate([h_re, h_im], axis=1).astype(BF16)
    y = (jnp.dot(u, m_ref[...], preferred_element_type=F32)
         + jnp.dot(h, v_ref[...], preferred_element_type=F32))
    yt_ref[...] = y.T


def _s5_scan(ut, w, m, v, a_t, n_seq):
    groups, vec, rows = ut.shape
    n_chunk = rows // n_seq
    return pl.pallas_call(
        functools.partial(_s5_kernel, n_seq=n_seq, n_chunk=n_chunk),
        grid=(groups,),
        in_specs=[
            pl.BlockSpec((None, vec, rows), lambda g: (g, 0, 0)),
            pl.BlockSpec((None, vec, 2 * S5_HALF), lambda g: (g, 0, 0)),
            pl.BlockSpec((None, vec, vec), lambda g: (g, 0, 0)),
            pl.BlockSpec((None, 2 * S5_HALF, vec), lambda g: (g, 0, 0)),
            pl.BlockSpec((None, 2, S5_HALF), lambda g: (g, 0, 0)),
        ],
        out_specs=pl.BlockSpec((None, vec, rows), lambda g: (g, 0, 0)),
        out_shape=jax.ShapeDtypeStruct((groups, vec, rows), F32),
        scratch_shapes=[
            pltpu.VMEM((2, rows, S5_HALF), F32),
            pltpu.VMEM((2, rows, S5_HALF), F32),
            pltpu.VMEM((2, rows, S5_HALF), F32),
        ],
        compiler_params=_params("arbitrary"),
        name="s5_scan",
    )(ut, w, m, v, a_t)


GROUPS_PER_LANE_BLOCK = LANES // S5_GROUP


def _to_groups_kernel(x_ref, o_ref):
    n_chunk = o_ref.shape[2]
    for tau in range(S5_CHUNK):
        rows = x_ref[pl.ds(tau, n_chunk, stride=S5_CHUNK), :]
        o_ref[:, tau * S5_GROUP:(tau + 1) * S5_GROUP, :] = rows.T.reshape(GROUPS_PER_LANE_BLOCK, S5_GROUP, n_chunk)


def _from_groups_kernel(y_ref, o_ref):
    n_chunk = y_ref.shape[2]
    for tau in range(S5_CHUNK):
        slab = y_ref[:, tau * S5_GROUP:(tau + 1) * S5_GROUP, :].reshape(LANES, n_chunk)
        o_ref[pl.ds(tau, n_chunk, stride=S5_CHUNK), :] = slab.T


def _s5_to_groups(x, width, chunks_per_step):
    tokens = x.shape[0]
    n_chunk = tokens // S5_CHUNK
    return pl.pallas_call(
        _to_groups_kernel,
        grid=(n_chunk // chunks_per_step, width // LANES),
        in_specs=[pl.BlockSpec((chunks_per_step * S5_CHUNK, LANES), lambda i, c: (i, c))],
        out_specs=pl.BlockSpec((GROUPS_PER_LANE_BLOCK, S5_VEC, chunks_per_step), lambda i, c: (c, 0, i)),
        out_shape=jax.ShapeDtypeStruct((width // S5_GROUP, S5_VEC, n_chunk), F32),
        compiler_params=_params("arbitrary", "arbitrary"),
        name="s5_to_groups",
    )(x)


def _s5_from_groups(yt, chunks_per_step):
    groups, _, n_chunk = yt.shape
    width = groups * S5_GROUP
    return pl.pallas_call(
        _from_groups_kernel,
        grid=(n_chunk // chunks_per_step, width // LANES),
        in_specs=[pl.BlockSpec((GROUPS_PER_LANE_BLOCK, S5_VEC, chunks_per_step), lambda i, c: (c, 0, i))],
        out_specs=pl.BlockSpec((chunks_per_step * S5_CHUNK, LANES), lambda i, c: (i, c)),
        out_shape=jax.ShapeDtypeStruct((n_chunk * S5_CHUNK, width), F32),
        compiler_params=_params("arbitrary", "arbitrary"),
        name="s5_from_groups",
    )(yt)


def _s5_operators(lam_re, lam_im, log_dt, b_re, b_im, c_re, c_im, d_skip):
    hi = lax.Precision.HIGHEST
    t = S5_CHUNK
    groups = lam_re.shape[1]
    lr = jnp.minimum(lam_re.astype(F32), -1e-4)
    li = lam_im.astype(F32)
    dt = jnp.exp(log_dt.astype(F32))[..., None]
    mag = jnp.exp(lr * dt)
    ab_re = mag * jnp.cos(li * dt)
    ab_im = mag * jnp.sin(li * dt)
    nr, ni = ab_re - 1.0, ab_im
    den = lr * lr + li * li
    f_re = (nr * lr + ni * li) / den
    f_im = (ni * lr - nr * li) / den
    br, bi = b_re.astype(F32), b_im.astype(F32)
    bb_re = f_re[..., None] * br - f_im[..., None] * bi
    bb_im = f_re[..., None] * bi + f_im[..., None] * br
    cr, ci = c_re.astype(F32), c_im.astype(F32)

    k = jnp.arange(t + 1, dtype=F32)
    mag_k = jnp.exp((lr * dt)[..., None] * k)
    ang_k = (li * dt)[..., None] * k
    pr = mag_k * jnp.cos(ang_k)
    pi = mag_k * jnp.sin(ang_k)

    ab_r = pr[..., :t, None] * bb_re[..., None, :] - pi[..., :t, None] * bb_im[..., None, :]
    ab_i = pr[..., :t, None] * bb_im[..., None, :] + pi[..., :t, None] * bb_re[..., None, :]
    lag = (jnp.einsum('zgnp,zgpkm->zgknm', cr, ab_r, precision=hi)
           - jnp.einsum('zgnp,zgpkm->zgknm', ci, ab_i, precision=hi))

    k_idx = np.arange(t)[:, None, None]
    s_idx = np.arange(t)[None, :, None]
    t_idx = np.arange(t)[None, None, :]
    sel_f = jnp.asarray((t_idx - s_idx == k_idx).astype(np.float32))
    sel_b = jnp.asarray((s_idx - t_idx == k_idx).astype(np.float32))
    skip = jnp.einsum('st,gn,nm->gsmtn', jnp.eye(t, dtype=F32), d_skip.astype(F32).reshape(groups, S5_GROUP),
                      jnp.eye(S5_GROUP, dtype=F32), precision=hi)
    m_op = (jnp.einsum('kst,gknm->gsmtn', sel_f, lag[0])
            + jnp.einsum('kst,gknm->gsmtn', sel_b, lag[1])
            + skip).reshape(groups, S5_VEC, S5_VEC)

    pf_r, pf_i = pr[0][..., :t][..., ::-1], pi[0][..., :t][..., ::-1]
    pb_r, pb_i = pr[1][..., :t], pi[1][..., :t]

    def contrib(p_r, p_i, z):
        re = p_r[..., None] * bb_re[z][:, :, None, :] - p_i[..., None] * bb_im[z][:, :, None, :]
        im = p_r[..., None] * bb_im[z][:, :, None, :] + p_i[..., None] * bb_re[z][:, :, None, :]
        to_rows = lambda x: x.transpose(0, 2, 3, 1).reshape(groups, S5_VEC, S5_STATE)
        return to_rows(re), to_rows(im)

    wf_re, wf_im = contrib(pf_r, pf_i, 0)
    wb_re, wb_im = contrib(pb_r, pb_i, 1)
    w_op = jnp.concatenate([wf_re, wb_re, wf_im, wb_im], axis=-1)

    qf_r, qf_i = pr[0][..., 1:], pi[0][..., 1:]
    qb_r, qb_i = pr[1][..., 1:][..., ::-1], pi[1][..., 1:][..., ::-1]

    def readout(q_r, q_i, z):
        c_r = cr[z].transpose(0, 2, 1)[:, :, None, :]
        c_i = ci[z].transpose(0, 2, 1)[:, :, None, :]
        w_r = c_r * q_r[..., None] - c_i * q_i[..., None]
        w_i = c_r * q_i[..., None] + c_i * q_r[..., None]
        return w_r.reshape(groups, S5_STATE, S5_VEC), (-w_i).reshape(groups, S5_STATE, S5_VEC)

    vf_re, vf_im = readout(qf_r, qf_i, 0)
    vb_re, vb_im = readout(qb_r, qb_i, 1)
    v_op = jnp.concatenate([vf_re, vb_re, vf_im, vb_im], axis=1)

    a_t = jnp.stack([jnp.concatenate([pr[0][..., t], pr[1][..., t]], axis=-1),
                     jnp.concatenate([pi[0][..., t], pi[1][..., t]], axis=-1)], axis=1)
    return w_op.astype(BF16), m_op.astype(BF16), v_op.astype(BF16), a_t


def _rope(x, cos, sin_lo, sin_hi):
    return (x * cos
            + pltpu.roll(x, HEAD_DIM - ROPE_HALF, 1) * sin_lo
            + pltpu.roll(x, ROPE_HALF, 1) * sin_hi)


def _attn_kernel(sink_ref, q_ref, k_ref, v_ref, qn_ref, kn_ref, cos_ref, slo_ref, shi_ref,
                 o_ref, kb_ref, vb_ref, *, seq, blocks_per_step):
    h = pl.program_id(1)
    n = pl.program_id(2)
    prep_rows = 256

    @pl.when(n == 0)
    def _():
        pad = jnp.zeros((BLOCK, HEAD_DIM), BF16)
        kb_ref[0:BLOCK, :] = pad
        vb_ref[0:BLOCK, :] = pad
        kb_ref[BLOCK + seq:, :] = pad
        vb_ref[BLOCK + seq:, :] = pad

        def prep(c, carry):
            r0 = pl.multiple_of(c * prep_rows, prep_rows)
            rows = pl.ds(r0, prep_rows)
            k = _rms_normalize(k_ref[rows, :], kn_ref[...])
            k = _rope(k, cos_ref[rows, :], slo_ref[rows, :], shi_ref[rows, :])
            kb_ref[pl.ds(r0 + BLOCK, prep_rows), :] = k.astype(BF16)
            vb_ref[pl.ds(r0 + BLOCK, prep_rows), :] = v_ref[rows, :].astype(BF16)
            return carry

        lax.fori_loop(0, seq // prep_rows, prep, 0)

    qi = lax.broadcasted_iota(jnp.int32, (BLOCK, 1), 0)
    kj = lax.broadcasted_iota(jnp.int32, (BLOCK, 3 * BLOCK), 1)

    def query_block(qb, carry):
        local = pl.multiple_of(qb * BLOCK, BLOCK)
        r0 = pl.multiple_of((n * blocks_per_step + qb) * BLOCK, BLOCK)
        rows = pl.ds(r0, BLOCK)
        cos, slo, shi = cos_ref[rows, :], slo_ref[rows, :], shi_ref[rows, :]
        heads = []
        for g in range(Q_PER_KV):
            qg = _rms_normalize(q_ref[pl.ds(local, BLOCK), g * HEAD_DIM:(g + 1) * HEAD_DIM], qn_ref[...])
            heads.append(_rope(qg, cos, slo, shi).astype(BF16))
        q4 = jnp.concatenate(heads, axis=0)
        kw = kb_ref[pl.ds(r0, 3 * BLOCK), :]
        vw = vb_ref[pl.ds(r0, 3 * BLOCK), :]
        s = lax.dot_general(q4, kw, (((1,), (1,)), ((), ())), preferred_element_type=F32)
        s = s * (HEAD_DIM ** -0.5)

        lo = jnp.maximum(qi + (BLOCK - WINDOW), BLOCK - r0)
        hi = jnp.minimum(qi + (BLOCK + WINDOW), seq - 1 + BLOCK - r0)
        probs = []
        inv_denoms = []
        for g in range(Q_PER_KV):
            sg = s[g * BLOCK:(g + 1) * BLOCK, :]
            sg = jnp.where(kj >= lo, jnp.where(kj <= hi, sg, NEG_INF), NEG_INF)
            sk = sink_ref[h * Q_PER_KV + g]
            mx = jnp.maximum(jnp.max(sg, axis=-1, keepdims=True), sk)
            p = jnp.exp(sg - mx)
            denom = jnp.sum(p, axis=-1, keepdims=True) + jnp.exp(sk - mx)
            probs.append(p.astype(BF16))
            inv_denoms.append(1.0 / denom)
        p4 = jnp.concatenate(probs, axis=0)
        o4 = jnp.dot(p4, vw, preferred_element_type=F32)
        for g in range(Q_PER_KV):
            og = o4[g * BLOCK:(g + 1) * BLOCK, :] * inv_denoms[g]
            o_ref[pl.ds(local, BLOCK), g * HEAD_DIM:(g + 1) * HEAD_DIM] = og.astype(o_ref.dtype)
        return carry

    lax.fori_loop(0, blocks_per_step, query_block, 0)


def _attention(proj, sink, q_norm, k_norm, cos, sin_lo, sin_hi, q_col0, k_col0, v_col0, blocks_per_step):
    bsz, seq, _ = proj.shape
    q_w = Q_PER_KV * HEAD_DIM
    q_blk0, k_blk0, v_blk0 = q_col0 // q_w, k_col0 // HEAD_DIM, v_col0 // HEAD_DIM
    table = pl.BlockSpec((seq, HEAD_DIM), lambda b, h, n: (0, 0))
    step_rows = blocks_per_step * BLOCK
    return pl.pallas_call(
        functools.partial(_attn_kernel, seq=seq, blocks_per_step=blocks_per_step),
        grid=(bsz, N_KV_HEADS, seq // step_rows),
        in_specs=[
            pl.BlockSpec(memory_space=pltpu.SMEM),
            pl.BlockSpec((None, step_rows, q_w), lambda b, h, n: (b, n, q_blk0 + h)),
            pl.BlockSpec((None, seq, HEAD_DIM), lambda b, h, n: (b, 0, k_blk0 + h)),
            pl.BlockSpec((None, seq, HEAD_DIM), lambda b, h, n: (b, 0, v_blk0 + h)),
            pl.BlockSpec((1, HEAD_DIM), lambda b, h, n: (0, 0)),
            pl.BlockSpec((1, HEAD_DIM), lambda b, h, n: (0, 0)),
            table, table, table,
        ],
        out_specs=pl.BlockSpec((None, step_rows, q_w), lambda b, h, n: (b, n, h)),
        out_shape=jax.ShapeDtypeStruct((bsz, seq, N_KV_HEADS * q_w), BF16),
        scratch_shapes=[
            pltpu.VMEM((seq + 2 * BLOCK, HEAD_DIM), BF16),
            pltpu.VMEM((seq + 2 * BLOCK, HEAD_DIM), BF16),
        ],
        compiler_params=_params("arbitrary", "arbitrary", "arbitrary"),
        name="window_attention",
    )(sink, proj, proj, proj, q_norm.reshape(1, HEAD_DIM), k_norm.reshape(1, HEAD_DIM), cos, sin_lo, sin_hi)


def _rope_tables(seq):
    inv = ROPE_THETA ** (-jnp.arange(ROPE_HALF, dtype=F32) / ROPE_HALF)
    ang = jnp.arange(seq).astype(F32)[:, None] * inv[None, :]
    cos, sin = jnp.cos(ang), jnp.sin(ang)
    rest = HEAD_DIM - 2 * ROPE_HALF
    cos_t = jnp.concatenate([cos, cos, jnp.ones((seq, rest), F32)], axis=-1)
    sin_lo = jnp.concatenate([-sin, jnp.zeros((seq, HEAD_DIM - ROPE_HALF), F32)], axis=-1)
    sin_hi = jnp.concatenate([jnp.zeros((seq, ROPE_HALF), F32), sin, jnp.zeros((seq, rest), F32)], axis=-1)
    return cos_t, sin_lo, sin_hi


def _rglru_kernel(gate_ref, xb_ref, cw_ref, cb_ref, wf_ref, wb_ref, bias_ref, lam_ref, o_ref,
                  xi_ref, conv_ref, hf_ref, pf_ref, hb_ref, pb_ref, af_ref, bf_ref, ab_ref, bb_ref, *, seq, tch):
    seg_len = seq // SUBLANES
    n_slab = LRU_BS // LANES
    halo = CONV_LEFT * SUBLANES
    n_chunk = seg_len // tch
    chunk_rows = tch * SUBLANES

    def interleave(tb, carry):
        t0 = tb * SUBLANES
        for s in range(SUBLANES):
            tile = xb_ref[pl.ds(pl.multiple_of(s * seg_len + t0, SUBLANES), SUBLANES), :]
            dst = pl.ds(halo + t0 * SUBLANES + s, SUBLANES, stride=SUBLANES)
            for slab in range(n_slab):
                xi_ref[slab, dst, :] = tile[:, slab * LANES:(slab + 1) * LANES]
        return carry

    lax.fori_loop(0, seg_len // SUBLANES, interleave, 0)

    sub = lax.broadcasted_iota(jnp.int32, (SUBLANES, LANES), 0)
    for slab in range(n_slab):
        for back in range(1, CONV_LEFT + 1):
            src = halo + (seg_len - back) * SUBLANES
            tile = pltpu.roll(xi_ref[slab, src:src + SUBLANES, :], 1, 0)
            xi_ref[slab, halo - back * SUBLANES:halo - (back - 1) * SUBLANES, :] = jnp.where(sub == 0, 0.0, tile)
        tile = pltpu.roll(xi_ref[slab, halo:halo + SUBLANES, :], SUBLANES - 1, 0)
        end = halo + seq
        xi_ref[slab, end:end + SUBLANES, :] = jnp.where(sub == SUBLANES - 1, 0.0, tile)

    cw = cw_ref[...]
    cb = cb_ref[...]

    def conv_chunk(c, carry):
        t0 = c * tch
        parts = []
        for slab in range(n_slab):
            cols = slice(slab * LANES, (slab + 1) * LANES)
            acc = cb[:, cols]
            for tap in range(CONV_WIDTH):
                r0 = pl.multiple_of((t0 + tap) * SUBLANES, SUBLANES)
                acc = acc + cw[tap:tap + 1, cols] * xi_ref[slab, pl.ds(r0, chunk_rows), :]
            parts.append(acc)
        conv_ref[pl.ds(pl.multiple_of(t0 * SUBLANES, SUBLANES), chunk_rows), :] = jnp.concatenate(parts, axis=1)
        return carry

    lax.fori_loop(0, n_chunk, conv_chunk, 0)

    def softplus(x):
        return jnp.maximum(x, 0.0) + jnp.log1p(jnp.exp(-jnp.abs(x)))

    def gates(t0, w_ref, z, a_out, b_out):
        half_conv = conv_ref[pl.ds(pl.multiple_of(t0 * SUBLANES, SUBLANES), chunk_rows), :]
        pre = jnp.dot(half_conv.astype(BF16), w_ref[...], preferred_element_type=F32)
        t_r = jnp.tanh(pre[:, :LRU_BS] + bias_ref[2 * z:2 * z + 1, :])
        t_i = jnp.tanh(pre[:, LRU_BS:] + bias_ref[2 * z + 1:2 * z + 2, :])
        c1 = (-0.5 * LRU_C) * softplus(-lam_ref[z:z + 1, :])
        a = jnp.exp(c1 * t_r + c1)
        a_out[...] = a
        v = 1.0 - a * a
        root = jnp.where(v > 0.0, v * lax.rsqrt(v), 0.0)
        b_out[...] = root * (half_conv * t_i + half_conv)

    def chunk_pair(c, carry):
        tf0 = c * tch
        tb0 = (n_chunk - 1 - c) * tch
        gates(tf0, wf_ref, 0, af_ref, bf_ref)
        gates(tb0, wb_ref, 1, ab_ref, bb_ref)

        def steps(k, state):
            hf, pf, hb, pb = state
            steps_per_trip = SUBLANES
            group = steps_per_trip * SUBLANES
            src_f = pl.ds(pl.multiple_of(k * group, group), group)
            src_b = pl.ds(pl.multiple_of(chunk_rows - group - k * group, group), group)
            dst_f = pl.ds(pl.multiple_of(tf0 * SUBLANES + k * group, group), group)
            dst_b = pl.ds(pl.multiple_of(tb0 * SUBLANES + chunk_rows - group - k * group, group), group)
            a_f, b_f = af_ref[src_f, :], bf_ref[src_f, :]
            a_b, b_b = ab_ref[src_b, :], bb_ref[src_b, :]
            hfs, pfs, hbs, pbs = [], [], [], []
            for j in range(steps_per_trip):
                up = slice(j * SUBLANES, (j + 1) * SUBLANES)
                down = slice((steps_per_trip - 1 - j) * SUBLANES, (steps_per_trip - j) * SUBLANES)
                hf = a_f[up] * hf + b_f[up]
                pf = a_f[up] * pf
                hb = a_b[down] * hb + b_b[down]
                pb = a_b[down] * pb
                hfs.append(hf)
                pfs.append(pf)
                hbs.insert(0, hb)
                pbs.insert(0, pb)
            hf_ref[dst_f, :] = jnp.concatenate(hfs, axis=0)
            pf_ref[dst_f, :] = jnp.concatenate(pfs, axis=0)
            hb_ref[dst_b, :] = jnp.concatenate(hbs, axis=0)
            pb_ref[dst_b, :] = jnp.concatenate(pbs, axis=0)
            return hf, pf, hb, pb

        return lax.fori_loop(0, tch // SUBLANES, steps, carry)

    zeros = jnp.zeros((SUBLANES, LRU_BS), F32)
    ones = jnp.ones((SUBLANES, LRU_BS), F32)
    end_f, prod_f, end_b, prod_b = lax.fori_loop(0, n_chunk, chunk_pair, (zeros, ones, zeros, ones))

    cin_f = [jnp.zeros((1, LRU_BS), F32)]
    for s in range(SUBLANES - 1):
        cin_f.append(end_f[s:s + 1, :] + prod_f[s:s + 1, :] * cin_f[s])
    cin_b = [jnp.zeros((1, LRU_BS), F32)]
    for s in range(SUBLANES - 1, 0, -1):
        cin_b.insert(0, end_b[s:s + 1, :] + prod_b[s:s + 1, :] * cin_b[0])
    def per_chunk(rows):
        tile = _rows_to_tile(rows, reverse=False)
        return jnp.broadcast_to(tile[None], (tch, SUBLANES, LRU_BS)).reshape(chunk_rows, LRU_BS)

    cin_f_tiles = per_chunk(cin_f)
    cin_b_tiles = per_chunk(cin_b)

    def fix_up(c, carry):
        rows = pl.ds(pl.multiple_of(c * chunk_rows, chunk_rows), chunk_rows)
        total = (hf_ref[rows, :] + pf_ref[rows, :] * cin_f_tiles) + (hb_ref[rows, :] + pb_ref[rows, :] * cin_b_tiles)
        for slab in range(n_slab):
            xi_ref[slab, rows, :] = total[:, slab * LANES:(slab + 1) * LANES]
        return carry

    lax.fori_loop(0, n_chunk, fix_up, 0)

    out_rows = 2 * SUBLANES

    def finish(tb, carry):
        for s in range(SUBLANES):
            halves = []
            for half in range(2):
                t0 = tb * out_rows + half * SUBLANES
                src = pl.ds(t0 * SUBLANES + s, SUBLANES, stride=SUBLANES)
                halves.append(jnp.concatenate([xi_ref[slab, src, :] for slab in range(n_slab)], axis=1))
            rows = pl.ds(pl.multiple_of(s * seg_len + tb * out_rows, out_rows), out_rows)
            y = jnp.concatenate(halves, axis=0) * jax.nn.gelu(gate_ref[rows, :])
            o_ref[rows, :] = y.astype(o_ref.dtype)
        return carry

    lax.fori_loop(0, seg_len // out_rows, finish, 0)


def _rglru(proj, conv_w, conv_b, w_fwd, w_bwd, bias, lam, tch):
    bsz, seq, width2 = proj.shape
    n_blk = width2 // (2 * LRU_BS)
    n_slab = LRU_BS // LANES
    seq_blk = lambda off: pl.BlockSpec((None, seq, LRU_BS), lambda b, n: (b, 0, off + n))
    per_blk = lambda rows, cols: pl.BlockSpec((None, rows, cols), lambda b, n: (n, 0, 0))
    slab_rows = pltpu.VMEM((seq, LRU_BS), F32)
    gate_rows = pltpu.VMEM((tch * SUBLANES, LRU_BS), F32)
    return pl.pallas_call(
        functools.partial(_rglru_kernel, seq=seq, tch=tch),
        grid=(bsz, n_blk),
        in_specs=[
            seq_blk(0),
            seq_blk(n_blk),
            pl.BlockSpec((CONV_WIDTH, LRU_BS), lambda b, n: (0, n)),
            pl.BlockSpec((1, LRU_BS), lambda b, n: (0, n)),
            per_blk(LRU_BS, 2 * LRU_BS),
            per_blk(LRU_BS, 2 * LRU_BS),
            per_blk(4, LRU_BS),
            per_blk(2, LRU_BS),
        ],
        out_specs=pl.BlockSpec((None, seq, LRU_BS), lambda b, n: (b, 0, n)),
        out_shape=jax.ShapeDtypeStruct((bsz, seq, n_blk * LRU_BS), BF16),
        scratch_shapes=[
            pltpu.VMEM((n_slab, seq + (CONV_WIDTH - 1) * SUBLANES, LANES), F32),
            pltpu.VMEM((seq, LRU_BS), F32),
            slab_rows, slab_rows, slab_rows, slab_rows,
            gate_rows, gate_rows, gate_rows, gate_rows,
        ],
        compiler_params=_params("arbitrary", "arbitrary"),
        name="rglru",
    )(proj, proj, conv_w, conv_b.reshape(1, -1), w_fwd, w_bwd, bias, lam)


def _even_layer(x, p, bsz, seq):
    m, d = x.shape
    s5_width = p["s5_w_glu"].shape[0]
    attn_width = N_KV_HEADS * Q_PER_KV * HEAD_DIM
    kv_width = N_KV_HEADS * HEAD_DIM
    proj = _norm_matmul(x, p["norm_mix"], p["w_in"], tm=ROW_TILE)

    ut = _s5_to_groups(proj, s5_width, chunks_per_step=RELAYOUT_CHUNKS)
    yt = _s5_scan(ut, p["s5_w"], p["s5_m"], p["s5_v"], p["s5_a"], n_seq=bsz)
    y = _s5_from_groups(yt, chunks_per_step=RELAYOUT_CHUNKS)
    y_s5 = _glu(y, p["s5_w_glu"], p["s5_b_glu"], tm=ROW_TILE)

    y_attn = _attention(proj.reshape(bsz, seq, -1), p["attn_sink"], p["attn_q_norm"], p["attn_k_norm"],
                        p["rope_cos"], p["rope_sin_lo"], p["rope_sin_hi"],
                        q_col0=s5_width, k_col0=s5_width + attn_width,
                        v_col0=s5_width + attn_width + kv_width, blocks_per_step=ATTN_BLOCKS_PER_STEP)
    return [y_s5, y_attn.reshape(m, attn_width)], [p["w_out"][:s5_width], p["w_out"][s5_width:]]


def _odd_layer(x, p, bsz, seq):
    m, d = x.shape
    proj = _norm_matmul(x, p["norm_mix"], p["w_in"], tm=ROW_TILE)
    y = _rglru(proj.reshape(bsz, seq, -1), p["conv_w"], p["conv_b"], p["w_fwd"], p["w_bwd"],
               p["gate_bias"], p["lam"], tch=LRU_TIME_CHUNK)
    return [y.reshape(m, -1)], [p["w_out"]]


def kernel(x_prompt, x_sample, norm_mix, norm_ffn, ev_w_in, ev_w_out, s5_lam_re, s5_lam_im, s5_log_dt, s5_b_re, s5_b_im, s5_c_re, s5_c_im, s5_d, s5_w_glu, s5_b_glu, attn_q_norm, attn_k_norm, attn_sink, od_w_in, od_w_out, lru_conv_w, lru_conv_b, lru_wa, lru_ba, lru_wx, lru_bx, lru_lam, ffn_w1, ffn_w3, ffn_w2):
    depth = norm_mix.shape[0]
    seq_lens = {x_prompt.shape[1], x_sample.shape[1]}
    rope = {s: _rope_tables(s) for s in seq_lens}

    layers = []
    for layer in range(depth):
        if layer % 2 == 0:
            e = layer // 2
            s5_w, s5_m, s5_v, s5_a = _s5_operators(s5_lam_re[e], s5_lam_im[e], s5_log_dt[e], s5_b_re[e],
                                                   s5_b_im[e], s5_c_re[e], s5_c_im[e], s5_d[e])
            mix = dict(norm_mix=norm_mix[layer], w_in=ev_w_in[e].astype(BF16), w_out=ev_w_out[e].astype(BF16),
                       s5_w=s5_w, s5_m=s5_m, s5_v=s5_v, s5_a=s5_a,
                       s5_w_glu=s5_w_glu[e].astype(BF16), s5_b_glu=s5_b_glu[e],
                       attn_q_norm=attn_q_norm[e], attn_k_norm=attn_k_norm[e], attn_sink=attn_sink[e])
        else:
            o = layer // 2
            n_blk = lru_wa.shape[2]
            mix = dict(norm_mix=norm_mix[layer], w_in=od_w_in[o].astype(BF16), w_out=od_w_out[o].astype(BF16),
                       conv_w=0.5 * lru_conv_w[o], conv_b=0.5 * lru_conv_b[o],
                       w_fwd=jnp.concatenate([lru_wa[o, 0], lru_wx[o, 0]], axis=-1).astype(BF16),
                       w_bwd=jnp.concatenate([lru_wa[o, 1], lru_wx[o, 1]], axis=-1).astype(BF16),
                       gate_bias=(0.5 * jnp.stack([lru_ba[o, 0], lru_bx[o, 0], lru_ba[o, 1], lru_bx[o, 1]], axis=0))
                       .reshape(4, n_blk, LRU_BS).transpose(1, 0, 2),
                       lam=lru_lam[o].reshape(2, n_blk, LRU_BS).transpose(1, 0, 2))
        layers.append(mix)
    w1_all, w3_all, w2_all = ffn_w1.astype(BF16), ffn_w3.astype(BF16), ffn_w2.astype(BF16)

    def run(x):
        bsz, seq, d = x.shape
        h = x.reshape(bsz * seq, d)
        for layer, mix in enumerate(layers):
            if layer % 2 == 0:
                cos, sin_lo, sin_hi = rope[seq]
                acts, wos = _even_layer(h, dict(mix, rope_cos=cos, rope_sin_lo=sin_lo, rope_sin_hi=sin_hi), bsz, seq)
            else:
                acts, wos = _odd_layer(h, mix, bsz, seq)
            h = _mix_ffn(h, acts, wos, norm_ffn[layer], w1_all, w3_all, w2_all, layer, tm=ROW_TILE, tf=FF_TILE)
        return h.reshape(bsz, seq, d)

    return (run(x_prompt), run(x_sample))
```
